```python
import math
import jax, jax.numpy as jnp
from jax import lax
import numpy as np


D_MODEL = 1024
BATCH = 2
SEQ = 8192
DEPTH = 1

SSD_EXPAND = 2
D_INNER = SSD_EXPAND * D_MODEL
SSD_HEAD_DIM = 64
SSD_HEADS = D_INNER // SSD_HEAD_DIM
SSD_GROUPS = 8
SSD_STATE = 128
CONV_K = 4
CONV_CH = D_INNER + 2 * SSD_GROUPS * SSD_STATE
CHUNK = 128
ATTN_HEADS = 8
ATTN_HEAD_DIM = D_MODEL // (2 * ATTN_HEADS)
ATTN_QK = ATTN_HEADS * 2 * ATTN_HEAD_DIM
ATTN_V = ATTN_HEADS * 2 * ATTN_HEAD_DIM
Q_BLOCK = 128
NUM_BUCKETS = 32
MAX_DISTANCE = 128
PEER_HEADS = 8
N_KEYS = 128
N_EXPERTS = N_KEYS * N_KEYS
PEER_DK = 128
PEER_TOPK = 16
TOKEN_BLOCK = 128
OFF_Z = 0
OFF_XBC = OFF_Z + D_INNER
OFF_DT = OFF_XBC + CONV_CH
OFF_Q = OFF_DT + SSD_HEADS
OFF_K = OFF_Q + ATTN_QK
OFF_V = OFF_K + ATTN_QK
OFF_GS = OFF_V + ATTN_V
OFF_GA = OFF_GS + D_MODEL
IN_COLS = OFF_GA + D_MODEL
EPS = 1e-6

kernel_name = 'hybrid_ssd_diffattn_peer_block'


def _rms(x):
    xf = x.astype(jnp.float32)
    return xf * lax.rsqrt(jnp.mean(xf * xf, axis=-1, keepdims=True) + EPS)


def rmsnorm(x, w):
    return (_rms(x) * w.astype(jnp.float32)).astype(x.dtype)


def t5_bucket(rel):
    n = jnp.maximum(-rel, 0)
    max_exact = NUM_BUCKETS // 2
    nf = jnp.maximum(n, 1).astype(jnp.float32)
    large = max_exact + (jnp.log(nf / max_exact) / math.log(MAX_DISTANCE / max_exact)
                         * (NUM_BUCKETS - max_exact)).astype(jnp.int32)
    large = jnp.minimum(large, NUM_BUCKETS - 1)
    return jnp.where(n < max_exact, n, large)


def causal_dwconv(u, w, bias):
    ch = u.shape[-1]
    y = lax.conv_general_dilated(u, w[:, None, :], window_strides=(1,),
                                 padding=[(CONV_K - 1, 0)],
                                 dimension_numbers=('NWC', 'WIO', 'NWC'),
                                 feature_group_count=ch)
    return y + bias


def ssd_scan(xh, dt, a, bm, cm):
    b, s, h, p = xh.shape
    g, n = bm.shape[2], bm.shape[3]
    r = h // g
    nc = s // CHUNK
    f32 = jnp.float32
    x = xh.reshape(b, nc, CHUNK, g, r, p).astype(f32)
    dtc = dt.reshape(b, nc, CHUNK, g, r)
    bc = bm.reshape(b, nc, CHUNK, g, n).astype(f32)
    cc = cm.reshape(b, nc, CHUNK, g, n).astype(f32)
    a_cum = jnp.cumsum(dtc * a.reshape(g, r), axis=2)
    causal = jnp.tril(jnp.ones((CHUNK, CHUNK), dtype=bool))
    seg = a_cum[:, :, :, None] - a_cum[:, :, None, :]
    decay = jnp.exp(jnp.where(causal[:, :, None, None], seg, -jnp.inf))
    cb = jnp.einsum('bclgn,bcsgn->bclsg', cc, bc)
    wts = cb[..., None] * decay * dtc[:, :, None]
    y_diag = jnp.einsum('bclsgr,bcsgrp->bclgrp', wts, x)
    to_end = jnp.exp(a_cum[:, :, -1:] - a_cum) * dtc
    states = jnp.einsum('bclgn,bclgr,bclgrp->bcgrpn', bc, to_end, x)
    chunk_decay = jnp.exp(a_cum[:, :, -1])

    def step(carry, inp):
        st, dec = inp
        return carry * dec[..., None, None] + st, carry

    init = jnp.zeros((b, g, r, p, n), f32)
    _, prev = lax.scan(step, init, (jnp.moveaxis(states, 1, 0), jnp.moveaxis(chunk_decay, 1, 0)))
    prev = jnp.moveaxis(prev, 0, 1)
    y_off = jnp.einsum('bclgn,bcgrpn,bclgr->bclgrp', cc, prev, jnp.exp(a_cum))
    return (y_diag + y_off).reshape(b, s, h, p).astype(xh.dtype)


def diff_attention(q, k, v, positions, bias_table, lam):
    b, s, h, _, d = q.shape
    nblk = s // Q_BLOCK
    qb = jnp.moveaxis(q.reshape(b, nblk, Q_BLOCK, h, 2, d), 1, 0)
    pb = jnp.moveaxis(positions.reshape(b, nblk, Q_BLOCK), 1, 0)
    kidx = jnp.arange(s)
    scale = d ** -0.5

    def block(args):
        qi, pi, bi = args
        qidx = bi * Q_BLOCK + jnp.arange(Q_BLOCK)
        logits = jnp.einsum('bqhid,bkhid->bhiqk', qi, k).astype(jnp.float32) * scale
        rel = positions[:, None, :] - pi[:, :, None]
        bias = bias_table[t5_bucket(rel)].astype(jnp.float32)
        logits = logits + jnp.transpose(bias, (0, 3, 1, 2))[:, :, None]
        mask = kidx[None, :] <= qidx[:, None]
        logits = jnp.where(mask, logits, -jnp.inf)
        pr = jax.nn.softmax(logits, axis=-1)
        attn = pr[:, :, 0] - lam * pr[:, :, 1]
        return jnp.einsum('bhqk,bkhv->bqhv', attn.astype(v.dtype), v)

    out = lax.map(block, (qb, pb, jnp.arange(nblk)))
    return jnp.moveaxis(out, 0, 1).reshape(b, s, h, -1)


def peer(xn, wq, key1, key2, u, v):
    b, s, dm = xn.shape
    toks = xn.reshape(-1, TOKEN_BLOCK, dm)
    half = PEER_DK // 2

    def block(xt):
        q = (xt @ wq).reshape(TOKEN_BLOCK, PEER_HEADS, 2, half).astype(jnp.float32)
        s1 = jnp.einsum('thd,hkd->thk', q[:, :, 0], key1.astype(jnp.float32))
        s2 = jnp.einsum('thd,hkd->thk', q[:, :, 1], key2.astype(jnp.float32))
        v1, i1 = lax.top_k(s1, PEER_TOPK)
        v2, i2 = lax.top_k(s2, PEER_TOPK)
        comb = (v1[..., :, None] + v2[..., None, :]).reshape(TOKEN_BLOCK, PEER_HEADS, PEER_TOPK * PEER_TOPK)
        sc, pos = lax.top_k(comb, PEER_TOPK)
        e1 = jnp.take_along_axis(i1, pos // PEER_TOPK, axis=-1)
        e2 = jnp.take_along_axis(i2, pos % PEER_TOPK, axis=-1)
        experts = e1 * N_KEYS + e2
        gate = jax.nn.softmax(sc, axis=-1)
        ue = u[experts]
        act = jax.nn.gelu(jnp.einsum('thkd,td->thk', ue, xt).astype(jnp.float32), approximate=False) * gate
        return jnp.einsum('thk,thkd->td', act.astype(xt.dtype), v[experts])

    return lax.map(block, toks).reshape(b, s, dm)


def setup_inputs(seed: int = 0) -> dict:
    key = jax.random.key(seed)
    ks = jax.random.split(key, 32)
    L = DEPTH
    f32 = jnp.float32

    def nrm(k, shape, std):
        return jax.random.normal(k, shape, f32) * std

    x = nrm(ks[0], (BATCH, SEQ, D_MODEL), 1.0)
    c = nrm(ks[1], (BATCH, D_MODEL), 1.0)
    positions = (jax.random.randint(ks[2], (BATCH, 1), 0, 1024, dtype=jnp.int32)
                 + jnp.arange(SEQ, dtype=jnp.int32)[None, :])
    ada_w = nrm(ks[3], (L, D_MODEL, 6 * D_MODEL), 0.02)
    ada_b = nrm(ks[4], (L, 6 * D_MODEL), 0.01)
    norm1_w = 1.0 + nrm(ks[5], (L, D_MODEL), 0.02)
    norm2_w = 1.0 + nrm(ks[6], (L, D_MODEL), 0.02)
    final_norm_w = 1.0 + nrm(ks[7], (D_MODEL,), 0.02)
    w_in = nrm(ks[8], (L, D_MODEL, IN_COLS), D_MODEL ** -0.5)
    conv_w = nrm(ks[9], (L, CONV_K, CONV_CH), CONV_K ** -0.5)
    conv_b = nrm(ks[10], (L, CONV_CH), 0.02)
    dt0 = jnp.exp(jax.random.uniform(ks[11], (L, SSD_HEADS), f32, math.log(1e-3), math.log(1e-1)))
    dt_bias = dt0 + jnp.log(-jnp.expm1(-dt0))
    a_log = jnp.log(jax.random.uniform(ks[12], (L, SSD_HEADS), f32, 1.0, 16.0))
    d_skip = 1.0 + nrm(ks[13], (L, SSD_HEADS), 0.02)
    ssd_norm_w = 1.0 + nrm(ks[14], (L, D_INNER), 0.02)
    w_ssd_o = nrm(ks[15], (L, D_INNER, D_MODEL), D_INNER ** -0.5)
    lambda_q1 = nrm(ks[16], (L, ATTN_HEAD_DIM), 0.1)
    lambda_k1 = nrm(ks[17], (L, ATTN_HEAD_DIM), 0.1)
    lambda_q2 = nrm(ks[18], (L, ATTN_HEAD_DIM), 0.1)
    lambda_k2 = nrm(ks[19], (L, ATTN_HEAD_DIM), 0.1)
    attn_norm_w = 1.0 + nrm(ks[20], (L, 2 * ATTN_HEAD_DIM), 0.02)
    w_attn_o = nrm(ks[21], (L, ATTN_V, D_MODEL), ATTN_V ** -0.5)
    rel_bias = nrm(ks[22], (NUM_BUCKETS, ATTN_HEADS), 0.5)
    w_out = nrm(ks[23], (L, D_MODEL, D_MODEL), D_MODEL ** -0.5)
    peer_wq = nrm(ks[24], (L, D_MODEL, PEER_HEADS * PEER_DK), D_MODEL ** -0.5)
    peer_key1 = nrm(ks[25], (L, PEER_HEADS, N_KEYS, PEER_DK // 2), (PEER_DK // 2) ** -0.5)
    peer_key2 = nrm(ks[26], (L, PEER_HEADS, N_KEYS, PEER_DK // 2), (PEER_DK // 2) ** -0.5)
    peer_u = nrm(ks[27], (L, N_EXPERTS, D_MODEL), D_MODEL ** -0.5)
    peer_v = nrm(ks[28], (L, N_EXPERTS, D_MODEL), 0.3)
    return {'x': x, 'c': c, 'positions': positions, 'ada_w': ada_w, 'ada_b': ada_b,
            'norm1_w': norm1_w, 'norm2_w': norm2_w, 'final_norm_w': final_norm_w,
            'w_in': w_in, 'conv_w': conv_w, 'conv_b': conv_b, 'dt_bias': dt_bias,
            'a_log': a_log, 'd_skip': d_skip, 'ssd_norm_w': ssd_norm_w, 'w_ssd_o': w_ssd_o,
            'lambda_q1': lambda_q1, 'lambda_k1': lambda_k1, 'lambda_q2': lambda_q2,
            'lambda_k2': lambda_k2, 'attn_norm_w': attn_norm_w, 'w_attn_o': w_attn_o,
            'rel_bias': rel_bias, 'w_out': w_out, 'peer_wq': peer_wq,
            'peer_key1': peer_key1, 'peer_key2': peer_key2, 'peer_u': peer_u, 'peer_v': peer_v}


def reference(x, c, positions, ada_w, ada_b, norm1_w, norm2_w, final_norm_w,
              w_in, conv_w, conv_b, dt_bias, a_log, d_skip, ssd_norm_w, w_ssd_o,
              lambda_q1, lambda_k1, lambda_q2, lambda_k2, attn_norm_w, w_attn_o,
              rel_bias, w_out, peer_wq, peer_key1, peer_key2, peer_u, peer_v):
    b, s, _ = x.shape
    f32 = jnp.float32
    h = x
    for l in range(DEPTH):
        lambda_init = 0.8 - 0.6 * math.exp(-0.3 * l)
        ada = jax.nn.silu(c) @ ada_w[l] + ada_b[l]
        sh1, sc1, g1, sh2, sc2, g2 = [m[:, None, :] for m in jnp.split(ada, 6, axis=-1)]

        n1 = rmsnorm(h, norm1_w[l]) * (1.0 + sc1) + sh1
        proj = n1 @ w_in[l]

        z = proj[..., OFF_Z:OFF_XBC]
        xbc = jax.nn.silu(causal_dwconv(proj[..., OFF_XBC:OFF_DT], conv_w[l], conv_b[l]))
        xs = xbc[..., :D_INNER].reshape(b, s, SSD_HEADS, SSD_HEAD_DIM)
        bm = xbc[..., D_INNER:D_INNER + SSD_GROUPS * SSD_STATE].reshape(b, s, SSD_GROUPS, SSD_STATE)
        cm = xbc[..., D_INNER + SSD_GROUPS * SSD_STATE:].reshape(b, s, SSD_GROUPS, SSD_STATE)
        dt = jax.nn.softplus(proj[..., OFF_DT:OFF_Q].astype(f32) + dt_bias[l].astype(f32))
        a = -jnp.exp(a_log[l].astype(f32))
        y = ssd_scan(xs, dt, a, bm, cm) + d_skip[l][:, None] * xs
        yz = (y.reshape(b, s, D_INNER) * jax.nn.silu(z)).reshape(b, s, SSD_GROUPS, D_INNER // SSD_GROUPS)
        y_ssd = (_rms(yz).reshape(b, s, D_INNER) * ssd_norm_w[l]).astype(x.dtype)
        y_ssd = y_ssd @ w_ssd_o[l]

        q = proj[..., OFF_Q:OFF_K].reshape(b, s, ATTN_HEADS, 2, ATTN_HEAD_DIM)
        k = proj[..., OFF_K:OFF_V].reshape(b, s, ATTN_HEADS, 2, ATTN_HEAD_DIM)
        v = proj[..., OFF_V:OFF_GS].reshape(b, s, ATTN_HEADS, 2 * ATTN_HEAD_DIM)
        lam = (jnp.exp(jnp.sum(lambda_q1[l].astype(f32) * lambda_k1[l].astype(f32)))
               - jnp.exp(jnp.sum(lambda_q2[l].astype(f32) * lambda_k2[l].astype(f32)))
               + lambda_init)
        att = diff_attention(q, k, v, positions, rel_bias, lam)
        att = rmsnorm(att, attn_norm_w[l]) * (1.0 - lambda_init)
        y_att = att.reshape(b, s, ATTN_V) @ w_attn_o[l]

        gate_s = jax.nn.sigmoid(proj[..., OFF_GS:OFF_GA])
        gate_a = jax.nn.sigmoid(proj[..., OFF_GA:IN_COLS])
        mix = (gate_s * y_ssd + gate_a * y_att) @ w_out[l]
        h = h + g1 * mix

        n2 = rmsnorm(h, norm2_w[l]) * (1.0 + sc2) + sh2
        h = h + g2 * peer(n2, peer_wq[l], peer_key1[l], peer_key2[l], peer_u[l], peer_v[l])
    return rmsnorm(h, final_norm_w)
```

```python
import functools
import math

import jax
import jax.numpy as jnp
from jax import lax
from jax.experimental import pallas as pl
from jax.experimental.pallas import tpu as pltpu

F32 = jnp.float32
BF16 = jnp.bfloat16
I32 = jnp.int32
HIGHEST = lax.Precision.HIGHEST

D_MODEL = 1024
D_INNER = 2048
SSD_HEADS = 32
SSD_HEAD_DIM = 64
SSD_GROUPS = 8
SSD_STATE = 128
CONV_K = 4
CONV_CH = D_INNER + 2 * SSD_GROUPS * SSD_STATE
CHUNK = 128
ATTN_HEADS = 8
ATTN_HEAD_DIM = 64
ATTN_V_DIM = 2 * ATTN_HEAD_DIM
NUM_BUCKETS = 32
MAX_DISTANCE = 128
PEER_HEADS = 8
N_KEYS = 128
N_EXPERTS = N_KEYS * N_KEYS
PEER_DK = 128
PEER_TOPK = 16
EPS = 1e-6
LAMBDA_INIT = 0.8 - 0.6 * math.exp(0.0)

LANES = 128
SUBLANES = 8
VMEM_LIMIT = 48 * 1024 * 1024

OFF_Z = 0
OFF_XBC = OFF_Z + D_INNER
OFF_DT = OFF_XBC + CONV_CH
OFF_Q = OFF_DT + SSD_HEADS
OFF_K = OFF_Q + ATTN_HEADS * 2 * ATTN_HEAD_DIM
OFF_V = OFF_K + ATTN_HEADS * 2 * ATTN_HEAD_DIM
OFF_GS = OFF_V + ATTN_HEADS * ATTN_V_DIM
OFF_GA = OFF_GS + D_MODEL
IN_COLS = OFF_GA + D_MODEL


def _t5_far_distance():
    max_exact = NUM_BUCKETS // 2
    n = max_exact
    while True:
        b = max_exact + int(math.log(n / max_exact) / math.log(MAX_DISTANCE / max_exact)
                            * (NUM_BUCKETS - max_exact))
        if b >= NUM_BUCKETS - 1:
            return n
        n += 1


N_FAR = _t5_far_distance()
assert N_FAR <= LANES


def _sigmoid(x):
    return 1.0 / (1.0 + jnp.exp(-x))


def _silu(x):
    return x * _sigmoid(x)


def _cparams(sem):
    return pltpu.CompilerParams(dimension_semantics=sem, vmem_limit_bytes=VMEM_LIMIT)


def _dot_nt(a, b):
    return lax.dot_general(a, b, (((1,), (1,)), ((), ())), preferred_element_type=F32)


def _ada_kernel(c_ref, w_ref, b_ref, o_ref):
    o_ref[...] = jnp.dot(_silu(c_ref[...]), w_ref[...], preferred_element_type=F32,
                         precision=HIGHEST) + b_ref[...]


def _ada(c8, w, b):
    n = w.shape[1]
    tn = D_MODEL
    return pl.pallas_call(
        _ada_kernel,
        grid=(n // tn,),
        in_specs=[pl.BlockSpec((SUBLANES, D_MODEL), lambda j: (0, 0)),
                  pl.BlockSpec((D_MODEL, tn), lambda j: (0, j)),
                  pl.BlockSpec((1, tn), lambda j: (0, j))],
        out_specs=pl.BlockSpec((SUBLANES, tn), lambda j: (0, j)),
        out_shape=jax.ShapeDtypeStruct((SUBLANES, n), F32),
        compiler_params=_cparams(("arbitrary",)),
        name="ada",
    )(c8, w, b)


def _modulated_norm(x, nw, sc, sh):
    r = x * lax.rsqrt(jnp.mean(x * x, axis=-1, keepdims=True) + EPS) * nw
    return r * (1.0 + sc) + sh


def _norm_proj_kernel(x_ref, nw_ref, sc_ref, sh_ref, w_ref, o_ref, xn_ref, *, precision):
    @pl.when(pl.program_id(2) == 0)
    def _():
        xn = _modulated_norm(x_ref[0], nw_ref[...], sc_ref[0], sh_ref[0])
        xn_ref[...] = xn.astype(xn_ref.dtype)

    o_ref[0] = jnp.dot(xn_ref[...], w_ref[...], preferred_element_type=F32,
                       precision=precision).astype(o_ref.dtype)


def _norm_proj(x, nw, sc, sh, w, out_dtype, tm, tn, name, precision=None):
    b, s, d = x.shape
    n = w.shape[1]
    tm = min(tm, s)
    tn = min(tn, n)
    return pl.pallas_call(
        functools.partial(_norm_proj_kernel, precision=precision),
        grid=(b, s // tm, n // tn),
        in_specs=[pl.BlockSpec((1, tm, d), lambda bi, m, j: (bi, m, 0)),
                  pl.BlockSpec((1, d), lambda bi, m, j: (0, 0)),
                  pl.BlockSpec((1, 1, d), lambda bi, m, j: (bi, 0, 0)),
                  pl.BlockSpec((1, 1, d), lambda bi, m, j: (bi, 0, 0)),
                  pl.BlockSpec((d, tn), lambda bi, m, j: (0, j))],
        out_specs=pl.BlockSpec((1, tm, tn), lambda bi, m, j: (bi, m, j)),
        out_shape=jax.ShapeDtypeStruct((b, s, n), out_dtype),
        scratch_shapes=[pltpu.VMEM((tm, d), w.dtype)],
        compiler_params=_cparams(("parallel", "parallel", "arbitrary")),
        name=name,
    )(x, nw, sc, sh, w)


CONV_HALO = SUBLANES
CONV_COLS = 512
GROUP_COLS = D_INNER // SSD_GROUPS
HEADS_PER_GROUP = SSD_HEADS // SSD_GROUPS
PF_COLS = CONV_CH + D_INNER + 2 * D_MODEL
PF_Z_BLOCK = CONV_CH // D_INNER
PF_GS_BLOCK = (CONV_CH + D_INNER) // D_MODEL
PF_GA_BLOCK = PF_GS_BLOCK + 1


def _softplus(x):
    return jnp.maximum(x, 0.0) + jnp.log(1.0 + jnp.exp(-jnp.abs(x)))


def _ssd_kernel(xbc_ref, z_ref, dt_ref, cw_ref, cb_ref, dtb_ref, alog_ref, dsk_ref, nw_ref,
                o_ref, state_ref, win_ref, act_ref):
    L = CHUNK

    @pl.when(pl.program_id(1) == 0)
    def _():
        state_ref[...] = jnp.zeros_like(state_ref)
        win_ref[0:CONV_HALO, :] = jnp.zeros((CONV_HALO, CONV_CH), F32)

    win_ref[CONV_HALO:CONV_HALO + L, :] = xbc_ref[0]
    base = CONV_HALO - (CONV_K - 1)
    for j in range(CONV_CH // CONV_COLS):
        cs = slice(j * CONV_COLS, (j + 1) * CONV_COLS)
        acc = cb_ref[:, cs] + cw_ref[0:1, cs] * win_ref[base:base + L, cs]
        for k in range(1, CONV_K):
            acc = acc + cw_ref[k:k + 1, cs] * win_ref[base + k:base + k + L, cs]
        act_ref[:, cs] = _silu(acc)
    win_ref[0:CONV_HALO, :] = win_ref[L:L + CONV_HALO, :]

    dtv = _softplus(dt_ref[0] + dtb_ref[...])
    a = -jnp.exp(alog_ref[...])
    row = lax.broadcasted_iota(I32, (L, L), 0)
    col = lax.broadcasted_iota(I32, (L, L), 1)
    tril = row >= col
    a_cum = jnp.dot(tril.astype(F32), dtv * a, preferred_element_type=F32,
                    precision=HIGHEST)
    a_cum_t = a_cum.T
    dt_t = dtv.T
    a_last = a_cum[L - 1:L, :]
    to_end = jnp.exp(a_last - a_cum) * dtv
    ea = jnp.exp(a_cum)
    cdec = jnp.exp(a_last)

    head_of_col = lax.broadcasted_iota(I32, (L, GROUP_COLS), 1) // SSD_HEAD_DIM
    neg_inf = jnp.float32(-jnp.inf)

    for g in range(SSD_GROUPS):
        xs = slice(g * GROUP_COLS, (g + 1) * GROUP_COLS)
        bs = slice(D_INNER + g * SSD_STATE, D_INNER + (g + 1) * SSD_STATE)
        cs = slice(D_INNER + (SSD_GROUPS + g) * SSD_STATE,
                   D_INNER + (SSD_GROUPS + g + 1) * SSD_STATE)
        xg = act_ref[:, xs]
        bg = act_ref[:, bs]
        xb = xg.astype(BF16)
        bb = bg.astype(BF16)
        cb16 = act_ref[:, cs].astype(BF16)
        cbm = _dot_nt(cb16, bb)
        prev_t = state_ref[:, xs]
        y_off = jnp.dot(cb16, prev_t.astype(BF16), preferred_element_type=F32)

        y = jnp.zeros((L, GROUP_COLS), F32)
        ea_g = jnp.zeros((L, GROUP_COLS), F32)
        te_g = jnp.zeros((L, GROUP_COLS), F32)
        cd_g = jnp.zeros((1, GROUP_COLS), F32)
        for r in range(HEADS_PER_GROUP):
            h = g * HEADS_PER_GROUP + r
            seg = a_cum[:, h:h + 1] - a_cum_t[h:h + 1, :]
            decay = jnp.exp(jnp.where(tril, seg, neg_inf))
            wts = (cbm * decay * dt_t[h:h + 1, :]).astype(BF16)
            yd = jnp.dot(wts, xb, preferred_element_type=F32)
            sel = head_of_col == r
            y = jnp.where(sel, yd, y)
            ea_g = jnp.where(sel, ea[:, h:h + 1], ea_g)
            te_g = jnp.where(sel, to_end[:, h:h + 1], te_g)
            cd_g = jnp.where(sel[0:1, :], cdec[:, h:h + 1], cd_g)
        y = y + y_off * ea_g + dsk_ref[:, xs] * xg
        xs_scaled = (xg * te_g).astype(BF16)
        st = jnp.dot(bg.T.astype(BF16), xs_scaled, preferred_element_type=F32)
        state_ref[:, xs] = prev_t * cd_g + st

        yz = y * _silu(z_ref[0, :, xs])
        ms = jnp.mean(yz * yz, axis=-1, keepdims=True)
        o_ref[0, :, xs] = (yz * lax.rsqrt(ms + EPS) * nw_ref[:, xs]).astype(o_ref.dtype)


def _ssd(pf, dtraw, cw, cb, dtb, alog, dsk, nw):
    b, s, _ = pf.shape
    nc = s // CHUNK
    const2 = lambda bi, c: (0, 0)
    return pl.pallas_call(
        _ssd_kernel,
        grid=(b, nc),
        in_specs=[pl.BlockSpec((1, CHUNK, CONV_CH), lambda bi, c: (bi, c, 0)),
                  pl.BlockSpec((1, CHUNK, D_INNER), lambda bi, c: (bi, c, PF_Z_BLOCK)),
                  pl.BlockSpec((1, CHUNK, LANES), lambda bi, c: (bi, c, 0)),
                  pl.BlockSpec((CONV_K, CONV_CH), const2),
                  pl.BlockSpec((1, CONV_CH), const2),
                  pl.BlockSpec((1, LANES), const2),
                  pl.BlockSpec((1, LANES), const2),
                  pl.BlockSpec((1, D_INNER), const2),
                  pl.BlockSpec((1, D_INNER), const2)],
        out_specs=pl.BlockSpec((1, CHUNK, D_INNER), lambda bi, c: (bi, c, 0)),
        out_shape=jax.ShapeDtypeStruct((b, s, D_INNER), BF16),
        scratch_shapes=[pltpu.VMEM((SSD_STATE, D_INNER), F32),
                        pltpu.VMEM((CONV_HALO + CHUNK, CONV_CH), F32),
                        pltpu.VMEM((CHUNK, CONV_CH), F32)],
        compiler_params=_cparams(("parallel", "arbitrary")),
        name="ssd",
    )(pf, pf, dtraw, cw, cb, dtb, alog, dsk, nw)


ATT_TILE = 512


def _t5_bias_lut(rb_row):
    n = lax.broadcasted_iota(I32, (LANES, LANES), 1)
    bkt = lax.broadcasted_iota(I32, (LANES, LANES), 0)
    max_exact = NUM_BUCKETS // 2
    nf = jnp.maximum(n, 1).astype(F32)
    large = max_exact + (jnp.log(nf / max_exact) / math.log(MAX_DISTANCE / max_exact)
                         * (NUM_BUCKETS - max_exact)).astype(I32)
    large = jnp.minimum(large, NUM_BUCKETS - 1)
    bucket = jnp.where(n < max_exact, n, large)
    onehot = (bucket == bkt).astype(F32)
    lut = jnp.dot(jnp.broadcast_to(rb_row, (SUBLANES, LANES)), onehot,
                  preferred_element_type=F32, precision=HIGHEST)
    return lut[0:1, :]


def _attn_kernel(pmin_ref, pmax_ref, q_ref, k_ref, v_ref, posc_ref, posr_ref, rb_ref,
                 lamv_ref, nw_ref, o_ref, qm_ref, m_ref, l_ref, acc_ref, *, tile):
    T = tile
    bi = pl.program_id(0)
    qi = pl.program_id(2)
    neg_inf = jnp.float32(-jnp.inf)

    q = q_ref[0]
    lane = lax.broadcasted_iota(I32, (T, 2 * ATTN_HEAD_DIM), 1)
    scale = ATTN_HEAD_DIM ** -0.5
    qs = q * jnp.asarray(scale, q.dtype)
    zero = jnp.zeros_like(qs)
    qm_ref[0] = jnp.where(lane < ATTN_HEAD_DIM, qs, zero)
    qm_ref[1] = jnp.where(lane >= ATTN_HEAD_DIM, qs, zero)
    m_ref[...] = jnp.full_like(m_ref, neg_inf)
    l_ref[...] = jnp.zeros_like(l_ref)
    acc_ref[...] = jnp.zeros_like(acc_ref)

    lut = _t5_bias_lut(rb_ref[0])
    c_far = lut[:, LANES - 1:LANES]
    lut_b = jnp.broadcast_to(lut, (T, LANES))
    pos_q = posc_ref[0]

    def tile_step(ki, near, diag):
        koff = pl.multiple_of(ki * T, T)
        k = k_ref[0, pl.ds(koff, T), :]
        v = v_ref[0, pl.ds(koff, T), :]
        if near:
            pos_k = posr_ref[0, :, pl.ds(koff, T)]
            parts = []
            for c in range(T // LANES):
                dist = jnp.clip(pos_q - pos_k[:, c * LANES:(c + 1) * LANES], 0, LANES - 1)
                parts.append(jnp.take_along_axis(lut_b, dist, axis=1))
            bias = jnp.concatenate(parts, axis=1)
            if diag:
                r = lax.broadcasted_iota(I32, (T, T), 0)
                c = lax.broadcasted_iota(I32, (T, T), 1)
                bias = jnp.where(c <= r, bias, neg_inf)
        for i in range(2):
            s = _dot_nt(qm_ref[i], k)
            if near:
                s = s + bias
                m_tile = jnp.max(s, axis=-1, keepdims=True)
            else:
                m_tile = jnp.max(s, axis=-1, keepdims=True) + c_far
            m_prev = m_ref[i]
            m_new = jnp.maximum(m_prev, m_tile)
            shift = m_new if near else m_new - c_far
            p = jnp.exp(s - shift)
            alpha = jnp.exp(m_prev - m_new)
            l_ref[i] = alpha * l_ref[i] + jnp.sum(p, axis=-1, keepdims=True)
            acc_ref[i] = alpha * acc_ref[i] + jnp.dot(p.astype(v.dtype), v,
                                                      preferred_element_type=F32)
            m_ref[i] = m_new

    def body(ki, carry):
        near = pmin_ref[bi, qi] - pmax_ref[bi, ki] < N_FAR

        @pl.when(near)
        def _():
            tile_step(ki, True, False)

        @pl.when(jnp.logical_not(near))
        def _():
            tile_step(ki, False, False)

        return carry

    lax.fori_loop(0, qi, body, 0)
    tile_step(qi, True, True)

    lv = lamv_ref[...]
    lam = (jnp.exp(jnp.sum(lv[0:1, :] * lv[1:2, :], axis=-1, keepdims=True))
           - jnp.exp(jnp.sum(lv[2:3, :] * lv[3:4, :], axis=-1, keepdims=True))
           + LAMBDA_INIT)
    att = acc_ref[0] / l_ref[0] - lam * (acc_ref[1] / l_ref[1])
    ms = jnp.mean(att * att, axis=-1, keepdims=True)
    o_ref[0] = (att * lax.rsqrt(ms + EPS) * nw_ref[...] * (1.0 - LAMBDA_INIT)).astype(o_ref.dtype)


def _attn(qkv, positions, rel_bias, lamv, nw):
    b, s, _ = qkv.shape
    T = min(ATT_TILE, s)
    nq = s // T
    H = ATTN_HEADS
    pos_blk = positions.reshape(b, nq, T)
    pmin = jnp.min(pos_blk, axis=-1)
    pmax = jnp.max(pos_blk, axis=-1)
    posc = positions.reshape(b, s, 1)
    posr = positions.reshape(b, 1, s)
    rb = jnp.zeros((H, 1, LANES), F32).at[:, 0, :NUM_BUCKETS].set(rel_bias.T)
    grid_spec = pltpu.PrefetchScalarGridSpec(
        num_scalar_prefetch=2,
        grid=(b, H, nq),
        in_specs=[pl.BlockSpec((1, T, LANES), lambda bi, h, qi, *_: (bi, qi, h)),
                  pl.BlockSpec((1, s, LANES), lambda bi, h, qi, *_: (bi, 0, H + h)),
                  pl.BlockSpec((1, s, LANES), lambda bi, h, qi, *_: (bi, 0, 2 * H + h)),
                  pl.BlockSpec((1, T, 1), lambda bi, h, qi, *_: (bi, qi, 0)),
                  pl.BlockSpec((1, 1, s), lambda bi, h, qi, *_: (bi, 0, 0)),
                  pl.BlockSpec((1, 1, LANES), lambda bi, h, qi, *_: (h, 0, 0)),
                  pl.BlockSpec((SUBLANES, LANES), lambda bi, h, qi, *_: (0, 0)),
                  pl.BlockSpec((1, LANES), lambda bi, h, qi, *_: (0, 0))],
        out_specs=pl.BlockSpec((1, T, LANES), lambda bi, h, qi, *_: (bi, qi, h)),
        scratch_shapes=[pltpu.VMEM((2, T, LANES), BF16),
                        pltpu.VMEM((2, T, 1), F32),
                        pltpu.VMEM((2, T, 1), F32),
                        pltpu.VMEM((2, T, LANES), F32)])
    return pl.pallas_call(
        functools.partial(_attn_kernel, tile=T),
        grid_spec=grid_spec,
        out_shape=jax.ShapeDtypeStruct((b, s, H * ATTN_V_DIM), BF16),
        compiler_params=_cparams(("parallel", "parallel", "arbitrary")),
        name="attn",
    )(pmin, pmax, qkv, qkv, qkv, posc, posr, rb, lamv, nw)


def _merge_kernel(x_ref, ys_ref, ya_ref, gs_ref, ga_ref, g1_ref, ws_ref, wa_ref, wo_ref, o_ref):
    y_ssd = jnp.dot(ys_ref[0], ws_ref[...], preferred_element_type=F32)
    y_att = jnp.dot(ya_ref[0], wa_ref[...], preferred_element_type=F32)
    mix_in = _sigmoid(gs_ref[0]) * y_ssd + _sigmoid(ga_ref[0]) * y_att
    mix = jnp.dot(mix_in.astype(BF16), wo_ref[...], preferred_element_type=F32)
    o_ref[0] = x_ref[0] + g1_ref[0] * mix


def _merge(x, ys, ya, pf, g1, ws, wa, wo, tm=512):
    b, s, d = x.shape
    tm = min(tm, s)
    const2 = lambda bi, m: (0, 0)
    return pl.pallas_call(
        _merge_kernel,
        grid=(b, s // tm),
        in_specs=[pl.BlockSpec((1, tm, d), lambda bi, m: (bi, m, 0)),
                  pl.BlockSpec((1, tm, D_INNER), lambda bi, m: (bi, m, 0)),
                  pl.BlockSpec((1, tm, d), lambda bi, m: (bi, m, 0)),
                  pl.BlockSpec((1, tm, d), lambda bi, m: (bi, m, PF_GS_BLOCK)),
                  pl.BlockSpec((1, tm, d), lambda bi, m: (bi, m, PF_GA_BLOCK)),
                  pl.BlockSpec((1, 1, d), lambda bi, m: (bi, 0, 0)),
                  pl.BlockSpec((D_INNER, d), const2),
                  pl.BlockSpec((d, d), const2),
                  pl.BlockSpec((d, d), const2)],
        out_specs=pl.BlockSpec((1, tm, d), lambda bi, m: (bi, m, 0)),
        out_shape=jax.ShapeDtypeStruct((b, s, d), F32),
        compiler_params=_cparams(("parallel", "parallel")),
        name="merge",
    )(x, ys, ya, pf, pf, g1, ws, wa, wo)


TOPK_TOKENS = 256
N_INST = 2 * PEER_HEADS
PAIR_CANDS = [(a, c) for a in range(PEER_TOPK) for c in range(PEER_TOPK)
              if (a + 1) * (c + 1) <= PEER_TOPK]


def _topk_kernel(q_ref, kp_ref, e1_ref, e2_ref, g_ref, s_ref, v_ref, i_ref, cs_ref, ce_ref,
                 os_ref, oe_ref):
    tn = q_ref.shape[0]
    neg_inf = jnp.float32(-jnp.inf)
    s_ref[...] = _dot_nt(kp_ref[...], q_ref[...].astype(BF16))

    def level1(it, prev):
        best = jnp.full((N_INST, tn), neg_inf, F32)
        bidx = jnp.zeros((N_INST, tn), I32)
        for key in range(N_KEYS):
            rows = slice(key * N_INST, (key + 1) * N_INST)
            sk = jnp.where(prev == key, neg_inf, s_ref[rows, :])
            s_ref[rows, :] = sk
            upd = sk > best
            best = jnp.where(upd, sk, best)
            bidx = jnp.where(upd, key, bidx)
        v_ref[it] = best
        i_ref[it] = bidx
        return bidx

    lax.fori_loop(0, PEER_TOPK, level1, jnp.full((N_INST, tn), -1, I32))

    H = PEER_HEADS
    for ci, (a, c) in enumerate(PAIR_CANDS):
        cs_ref[ci] = v_ref[a, 0:H, :] + v_ref[c, H:2 * H, :]
        ce_ref[ci] = i_ref[a, 0:H, :] * N_KEYS + i_ref[c, H:2 * H, :]

    def level2(it, prev):
        best = jnp.full((H, tn), neg_inf, F32)
        bci = jnp.zeros((H, tn), I32)
        bex = jnp.zeros((H, tn), I32)
        for ci in range(len(PAIR_CANDS)):
            sk = jnp.where(prev == ci, neg_inf, cs_ref[ci])
            cs_ref[ci] = sk
            upd = sk > best
            best = jnp.where(upd, sk, best)
            bci = jnp.where(upd, ci, bci)
            bex = jnp.where(upd, ce_ref[ci], bex)
        os_ref[it] = best
        oe_ref[it] = bex
        return bci

    lax.fori_loop(0, PEER_TOPK, level2, jnp.full((H, tn), -1, I32))

    sc = os_ref[...]
    ex = jnp.exp(sc - sc[0:1])
    gate = ex / jnp.sum(ex, axis=0, keepdims=True)
    ex_i = oe_ref[...]
    rows = PEER_TOPK * H
    g_ref[...] = gate.reshape(rows, tn).T
    key_bits = N_KEYS.bit_length() - 1
    e1 = lax.shift_right_logical(ex_i, key_bits).astype(F32).reshape(rows, tn)
    e2 = lax.bitwise_and(ex_i, N_KEYS - 1).astype(F32).reshape(rows, tn)
    e1_ref[...] = e1.T.astype(I32)
    e2_ref[...] = e2.T.astype(I32)


def _peer_topk(q, kperm):
    t, d = q.shape
    tn = min(TOPK_TOKENS, t)
    lanes = PEER_TOPK * PEER_HEADS
    ncand = len(PAIR_CANDS)
    out = jax.ShapeDtypeStruct((t, lanes), I32)
    return pl.pallas_call(
        _topk_kernel,
        grid=(t // tn,),
        in_specs=[pl.BlockSpec((tn, d), lambda i: (i, 0)),
                  pl.BlockSpec(kperm.shape, lambda i: (0, 0))],
        out_specs=[pl.BlockSpec((tn, lanes), lambda i: (i, 0))] * 3,
        out_shape=[out, out, jax.ShapeDtypeStruct((t, lanes), F32)],
        scratch_shapes=[pltpu.VMEM((N_KEYS * N_INST, tn), F32),
                        pltpu.VMEM((PEER_TOPK, N_INST, tn), F32),
                        pltpu.VMEM((PEER_TOPK, N_INST, tn), I32),
                        pltpu.VMEM((ncand, PEER_HEADS, tn), F32),
                        pltpu.VMEM((ncand, PEER_HEADS, tn), I32),
                        pltpu.VMEM((PEER_TOPK, PEER_HEADS, tn), F32),
                        pltpu.VMEM((PEER_TOPK, PEER_HEADS, tn), I32)],
        compiler_params=_cparams(("parallel",)),
        name="peer_topk",
    )(q, kperm)


MIX_TOKENS = 256
MIX_EXPERTS = 1024
MIX_SUB = 256
G_PITCH = N_KEYS + SUBLANES


def _gelu_exact(x):
    return 0.5 * x * (1.0 + lax.erf(x * (1.0 / math.sqrt(2.0))))


def _mix_kernel(h_ref, nw_ref, sc_ref, sh_ref, g2_ref, fw_ref, e1_ref, e2_ref, gt_ref,
                u_ref, v_ref, o_ref, xb_ref, ghi_ref, glo_ref, g_ref, acc_ref):
    tb = h_ref.shape[1]
    c = pl.program_id(2)

    @pl.when(c == 0)
    def _():
        xn = _modulated_norm(h_ref[0], nw_ref[...], sc_ref[0], sh_ref[0])
        xb_ref[...] = xn.astype(BF16)
        acc_ref[...] = jnp.zeros_like(acc_ref)
        gate = gt_ref[...]
        ghi = gate.astype(BF16).astype(F32)
        ghi_ref[...] = ghi
        glo_ref[...] = gate - ghi
        sub = lax.broadcasted_iota(I32, (N_KEYS, LANES), 0)

        def build(t, carry):
            eq1 = sub == e1_ref[pl.ds(t, 1), :]
            eq2 = sub == e2_ref[pl.ds(t, 1), :]
            ghi_t = ghi_ref[pl.ds(t, 1), :]
            glo_t = glo_ref[pl.ds(t, 1), :]
            a = jnp.concatenate([jnp.where(eq1, ghi_t, 0.0), jnp.where(eq1, glo_t, 0.0)],
                                axis=1).astype(BF16)
            one = jnp.where(eq2, 1.0, 0.0)
            bm = jnp.concatenate([one, one], axis=1).astype(BF16)
            g_ref[pl.ds(pl.multiple_of(t * G_PITCH, SUBLANES), N_KEYS), :] = _dot_nt(a, bm)
            return carry

        lax.fori_loop(0, tb, build, 0)

    xb = xb_ref[...]
    per_key = MIX_SUB // N_KEYS
    for j in range(MIX_EXPERTS // MIX_SUB):
        rows = slice(j * MIX_SUB, (j + 1) * MIX_SUB)
        hid = _dot_nt(xb, u_ref[rows, :])
        key0 = c * (MIX_EXPERTS // N_KEYS) + j * per_key
        gates = jnp.concatenate(
            [g_ref[pl.ds(key0 + i, tb, stride=G_PITCH), :] for i in range(per_key)], axis=1)
        p = (_gelu_exact(hid) * gates).astype(BF16)
        acc_ref[...] += jnp.dot(p, v_ref[rows, :], preferred_element_type=F32)

    @pl.when(c == pl.num_programs(2) - 1)
    def _():
        h2 = h_ref[0] + g2_ref[0] * acc_ref[...]
        r = h2 * lax.rsqrt(jnp.mean(h2 * h2, axis=-1, keepdims=True) + EPS)
        o_ref[0] = r * fw_ref[...]


def _peer_mix(h, nw, sc, sh, g2, fw, e1, e2, gate, u, v):
    b, s, d = h.shape
    tb = min(MIX_TOKENS, s)
    nb = s // tb
    ne = u.shape[0] // MIX_EXPERTS
    lanes = PEER_TOPK * PEER_HEADS
    tok = lambda bi, m, c: (bi * nb + m, 0)
    vec = lambda bi, m, c: (bi, 0, 0)
    return pl.pallas_call(
        _mix_kernel,
        grid=(b, nb, ne),
        in_specs=[pl.BlockSpec((1, tb, d), lambda bi, m, c: (bi, m, 0)),
                  pl.BlockSpec((1, d), lambda bi, m, c: (0, 0)),
                  pl.BlockSpec((1, 1, d), vec),
                  pl.BlockSpec((1, 1, d), vec),
                  pl.BlockSpec((1, 1, d), vec),
                  pl.BlockSpec((1, d), lambda bi, m, c: (0, 0)),
                  pl.BlockSpec((tb, lanes), tok),
                  pl.BlockSpec((tb, lanes), tok),
                  pl.BlockSpec((tb, lanes), tok),
                  pl.BlockSpec((MIX_EXPERTS, d), lambda bi, m, c: (c, 0)),
                  pl.BlockSpec((MIX_EXPERTS, d), lambda bi, m, c: (c, 0))],
        out_specs=pl.BlockSpec((1, tb, d), lambda bi, m, c: (bi, m, 0)),
        out_shape=jax.ShapeDtypeStruct((b, s, d), F32),
        scratch_shapes=[pltpu.VMEM((tb, d), BF16),
                        pltpu.VMEM((tb, lanes), F32),
                        pltpu.VMEM((tb, lanes), F32),
                        pltpu.VMEM((tb * G_PITCH, LANES), F32),
                        pltpu.VMEM((tb, d), F32)],
        compiler_params=_cparams(("parallel", "parallel", "arbitrary")),
        name="peer_mix",
    )(h, nw, sc, sh, g2, fw, e1, e2, gate, u, v)


def _pad_lanes(vec, width=LANES):
    return jnp.zeros((1, width), F32).at[0, :vec.shape[0]].set(vec)


def kernel(x, c, positions, ada_w, ada_b, norm1_w, norm2_w, final_norm_w, w_in, conv_w, conv_b,
           dt_bias, a_log, d_skip, ssd_norm_w, w_ssd_o, lambda_q1, lambda_k1, lambda_q2,
           lambda_k2, attn_norm_w, w_attn_o, rel_bias, w_out, peer_wq, peer_key1, peer_key2,
           peer_u, peer_v):
    b, s, d = x.shape
    l = 0

    c8 = jnp.zeros((SUBLANES, d), F32).at[:b].set(c)
    ada = _ada(c8, ada_w[l], ada_b[l][None, :])[:b]
    sh1, sc1, g1, sh2, sc2, g2 = [m[:, None, :] for m in jnp.split(ada, 6, axis=-1)]

    wl = w_in[l]
    w_f = jnp.concatenate([wl[:, OFF_XBC:OFF_DT], wl[:, OFF_Z:OFF_XBC], wl[:, OFF_GS:IN_COLS]],
                          axis=1).astype(BF16)
    w_qkv = wl[:, OFF_Q:OFF_GS].astype(BF16)
    w_dt = jnp.zeros((d, LANES), F32).at[:, :SSD_HEADS].set(wl[:, OFF_DT:OFF_Q])
    nw1 = norm1_w[l][None, :]

    pf = _norm_proj(x, nw1, sc1, sh1, w_f, F32, 512, 1024, "proj_f32")
    qkv = _norm_proj(x, nw1, sc1, sh1, w_qkv, BF16, 512, 1024, "proj_qkv")
    dtraw = _norm_proj(x, nw1, sc1, sh1, w_dt, F32, 512, LANES, "proj_dt", precision=HIGHEST)

    dsk = jnp.repeat(d_skip[l], SSD_HEAD_DIM)[None, :]
    yn = _ssd(pf, dtraw, conv_w[l], conv_b[l][None, :], _pad_lanes(dt_bias[l]),
              _pad_lanes(a_log[l]), dsk, ssd_norm_w[l][None, :])

    lamv = jnp.zeros((SUBLANES, LANES), F32)
    for i, v_ in enumerate((lambda_q1, lambda_k1, lambda_q2, lambda_k2)):
        lamv = lamv.at[i, :ATTN_HEAD_DIM].set(v_[l])
    an = _attn(qkv, positions, rel_bias, lamv, attn_norm_w[l][None, :])

    h1 = _merge(x, yn, an, pf, g1, w_ssd_o[l].astype(BF16), w_attn_o[l].astype(BF16),
                w_out[l].astype(BF16))

    nw2 = norm2_w[l][None, :]
    q = _norm_proj(h1, nw2, sc2, sh2, peer_wq[l].astype(BF16), F32, 512, 1024, "peer_q")
    keys = jnp.stack([peer_key1[l], peer_key2[l]])
    kperm = jnp.einsum('ahkd,ab,hg->kahgbd', keys, jnp.eye(2, dtype=F32),
                       jnp.eye(PEER_HEADS, dtype=F32))
    kperm = kperm.reshape(N_KEYS * N_INST, PEER_HEADS * PEER_DK).astype(BF16)
    e1, e2, gate = _peer_topk(q.reshape(b * s, PEER_HEADS * PEER_DK), kperm)
    return _peer_mix(h1, nw2, sc2, sh2, g2, final_norm_w[None, :], e1, e2, gate,
                     peer_u[l].astype(BF16), peer_v[l].astype(BF16))
```

```python
import functools
import math

import jax
import jax.numpy as jnp
from jax import lax
from jax.experimental import pallas as pl
from jax.experimental.pallas import tpu as pltpu

F32 = jnp.float32
BF16 = jnp.bfloat16
I32 = jnp.int32
HIGHEST = lax.Precision.HIGHEST

D_MODEL = 1024
D_INNER = 2048
SSD_HEADS = 32
SSD_HEAD_DIM = 64
SSD_GROUPS = 8
SSD_STATE = 128
CONV_K = 4
CONV_CH = D_INNER + 2 * SSD_GROUPS * SSD_STATE
CHUNK = 128
ATTN_HEADS = 8
ATTN_HEAD_DIM = 64
ATTN_V_DIM = 2 * ATTN_HEAD_DIM
NUM_BUCKETS = 32
MAX_DISTANCE = 128
PEER_HEADS = 8
N_KEYS = 128
N_EXPERTS = N_KEYS * N_KEYS
PEER_DK = 128
PEER_TOPK = 16
EPS = 1e-6
LAMBDA_INIT = 0.8 - 0.6 * math.exp(0.0)

LANES = 128
SUBLANES = 8
VMEM_LIMIT = 48 * 1024 * 1024

OFF_Z = 0
OFF_XBC = OFF_Z + D_INNER
OFF_DT = OFF_XBC + CONV_CH
OFF_Q = OFF_DT + SSD_HEADS
OFF_K = OFF_Q + ATTN_HEADS * 2 * ATTN_HEAD_DIM
OFF_V = OFF_K + ATTN_HEADS * 2 * ATTN_HEAD_DIM
OFF_GS = OFF_V + ATTN_HEADS * ATTN_V_DIM
OFF_GA = OFF_GS + D_MODEL
IN_COLS = OFF_GA + D_MODEL


def _t5_far_distance():
    max_exact = NUM_BUCKETS // 2
    n = max_exact
    while True:
        b = max_exact + int(math.log(n / max_exact) / math.log(MAX_DISTANCE / max_exact)
                            * (NUM_BUCKETS - max_exact))
        if b >= NUM_BUCKETS - 1:
            return n
        n += 1


N_FAR = _t5_far_distance()
assert N_FAR <= LANES


def _sigmoid(x):
    return 1.0 / (1.0 + jnp.exp(-x))


def _silu(x):
    return x * _sigmoid(x)


def _cparams(sem):
    return pltpu.CompilerParams(dimension_semantics=sem, vmem_limit_bytes=VMEM_LIMIT)


def _dot_nt(a, b):
    return lax.dot_general(a, b, (((1,), (1,)), ((), ())), preferred_element_type=F32)


def _ada_kernel(c_ref, w_ref, b_ref, o_ref):
    o_ref[...] = jnp.dot(_silu(c_ref[...]), w_ref[...], preferred_element_type=F32,
                         precision=HIGHEST) + b_ref[...]


def _ada(c8, w, b):
    n = w.shape[1]
    tn = D_MODEL
    return pl.pallas_call(
        _ada_kernel,
        grid=(n // tn,),
        in_specs=[pl.BlockSpec((SUBLANES, D_MODEL), lambda j: (0, 0)),
                  pl.BlockSpec((D_MODEL, tn), lambda j: (0, j)),
                  pl.BlockSpec((1, tn), lambda j: (0, j))],
        out_specs=pl.BlockSpec((SUBLANES, tn), lambda j: (0, j)),
        out_shape=jax.ShapeDtypeStruct((SUBLANES, n), F32),
        compiler_params=_cparams(("arbitrary",)),
        name="ada",
    )(c8, w, b)


def _modulated_norm(x, nw, sc, sh):
    r = x * lax.rsqrt(jnp.mean(x * x, axis=-1, keepdims=True) + EPS) * nw
    return r * (1.0 + sc) + sh


def _norm_proj_kernel(x_ref, nw_ref, sc_ref, sh_ref, w_ref, o_ref, xn_ref, *, precision):
    @pl.when(pl.program_id(2) == 0)
    def _():
        xn = _modulated_norm(x_ref[0], nw_ref[...], sc_ref[0], sh_ref[0])
        xn_ref[...] = xn.astype(xn_ref.dtype)

    o_ref[0] = jnp.dot(xn_ref[...], w_ref[...], preferred_element_type=F32,
                       precision=precision).astype(o_ref.dtype)


def _norm_proj(x, nw, sc, sh, w, out_dtype, tm, tn, name, precision=None):
    b, s, d = x.shape
    n = w.shape[1]
    tm = min(tm, s)
    tn = min(tn, n)
    return pl.pallas_call(
        functools.partial(_norm_proj_kernel, precision=precision),
        grid=(b, s // tm, n // tn),
        in_specs=[pl.BlockSpec((1, tm, d), lambda bi, m, j: (bi, m, 0)),
                  pl.BlockSpec((1, d), lambda bi, m, j: (0, 0)),
                  pl.BlockSpec((1, 1, d), lambda bi, m, j: (bi, 0, 0)),
                  pl.BlockSpec((1, 1, d), lambda bi, m, j: (bi, 0, 0)),
                  pl.BlockSpec((d, tn), lambda bi, m, j: (0, j))],
        out_specs=pl.BlockSpec((1, tm, tn), lambda bi, m, j: (bi, m, j)),
        out_shape=jax.ShapeDtypeStruct((b, s, n), out_dtype),
        scratch_shapes=[pltpu.VMEM((tm, d), w.dtype)],
        compiler_params=_cparams(("parallel", "parallel", "arbitrary")),
        name=name,
    )(x, nw, sc, sh, w)


def _norm_proj_t_kernel(x_ref, nw_ref, sc_ref, sh_ref, wt_ref, o_ref, xn_ref):
    @pl.when(pl.program_id(2) == 0)
    def _():
        xn = _modulated_norm(x_ref[0], nw_ref[...], sc_ref[0], sh_ref[0])
        xn_ref[...] = xn.astype(xn_ref.dtype)

    o_ref[0] = _dot_nt(wt_ref[...], xn_ref[...]).astype(o_ref.dtype)


def _norm_proj_t(x, nw, sc, sh, wt, out_dtype, tm, tn, name):
    b, s, d = x.shape
    n = wt.shape[0]
    tm = min(tm, s)
    tn = min(tn, n)
    return pl.pallas_call(
        _norm_proj_t_kernel,
        grid=(b, s // tm, n // tn),
        in_specs=[pl.BlockSpec((1, tm, d), lambda bi, m, j: (bi, m, 0)),
                  pl.BlockSpec((1, d), lambda bi, m, j: (0, 0)),
                  pl.BlockSpec((1, 1, d), lambda bi, m, j: (bi, 0, 0)),
                  pl.BlockSpec((1, 1, d), lambda bi, m, j: (bi, 0, 0)),
                  pl.BlockSpec((tn, d), lambda bi, m, j: (j, 0))],
        out_specs=pl.BlockSpec((1, tn, tm), lambda bi, m, j: (bi, j, m)),
        out_shape=jax.ShapeDtypeStruct((b, n, s), out_dtype),
        scratch_shapes=[pltpu.VMEM((tm, d), wt.dtype)],
        compiler_params=_cparams(("parallel", "parallel", "arbitrary")),
        name=name,
    )(x, nw, sc, sh, wt)


CONV_HALO = SUBLANES
CONV_COLS = 512
GROUP_COLS = D_INNER // SSD_GROUPS
HEADS_PER_GROUP = SSD_HEADS // SSD_GROUPS
PF_COLS = CONV_CH + D_INNER + 2 * D_MODEL
PF_Z_BLOCK = CONV_CH // D_INNER
PF_GS_BLOCK = (CONV_CH + D_INNER) // D_MODEL
PF_GA_BLOCK = PF_GS_BLOCK + 1


def _softplus(x):
    return jnp.maximum(x, 0.0) + jnp.log(1.0 + jnp.exp(-jnp.abs(x)))


def _ssd_kernel(xbc_ref, z_ref, dt_ref, cw_ref, cb_ref, dtb_ref, alog_ref, dsk_ref, nw_ref,
                o_ref, state_ref, win_ref, act_ref):
    L = CHUNK

    @pl.when(pl.program_id(1) == 0)
    def _():
        state_ref[...] = jnp.zeros_like(state_ref)
        win_ref[0:CONV_HALO, :] = jnp.zeros((CONV_HALO, CONV_CH), F32)

    win_ref[CONV_HALO:CONV_HALO + L, :] = xbc_ref[0]
    base = CONV_HALO - (CONV_K - 1)
    for j in range(CONV_CH // CONV_COLS):
        cs = slice(j * CONV_COLS, (j + 1) * CONV_COLS)
        acc = cb_ref[:, cs] + cw_ref[0:1, cs] * win_ref[base:base + L, cs]
        for k in range(1, CONV_K):
            acc = acc + cw_ref[k:k + 1, cs] * win_ref[base + k:base + k + L, cs]
        act_ref[:, cs] = _silu(acc)
    win_ref[0:CONV_HALO, :] = win_ref[L:L + CONV_HALO, :]

    dtv = _softplus(dt_ref[0] + dtb_ref[...])
    a = -jnp.exp(alog_ref[...])
    row = lax.broadcasted_iota(I32, (L, L), 0)
    col = lax.broadcasted_iota(I32, (L, L), 1)
    tril = row >= col
    a_cum = jnp.dot(tril.astype(F32), dtv * a, preferred_element_type=F32,
                    precision=HIGHEST)
    a_cum_t = a_cum.T
    dt_t = dtv.T
    a_last = a_cum[L - 1:L, :]
    to_end = jnp.exp(a_last - a_cum) * dtv
    ea = jnp.exp(a_cum)
    cdec = jnp.exp(a_last)

    head_of_col = lax.broadcasted_iota(I32, (L, GROUP_COLS), 1) // SSD_HEAD_DIM
    neg_inf = jnp.float32(-jnp.inf)

    for g in range(SSD_GROUPS):
        xs = slice(g * GROUP_COLS, (g + 1) * GROUP_COLS)
        bs = slice(D_INNER + g * SSD_STATE, D_INNER + (g + 1) * SSD_STATE)
        cs = slice(D_INNER + (SSD_GROUPS + g) * SSD_STATE,
                   D_INNER + (SSD_GROUPS + g + 1) * SSD_STATE)
        xg = act_ref[:, xs]
        bg = act_ref[:, bs]
        xb = xg.astype(BF16)
        bb = bg.astype(BF16)
        cb16 = act_ref[:, cs].astype(BF16)
        cbm = _dot_nt(cb16, bb)
        prev_t = state_ref[:, xs]
        y_off = jnp.dot(cb16, prev_t.astype(BF16), preferred_element_type=F32)

        y = jnp.zeros((L, GROUP_COLS), F32)
        ea_g = jnp.zeros((L, GROUP_COLS), F32)
        te_g = jnp.zeros((L, GROUP_COLS), F32)
        cd_g = jnp.zeros((1, GROUP_COLS), F32)
        for r in range(HEADS_PER_GROUP):
            h = g * HEADS_PER_GROUP + r
            seg = a_cum[:, h:h + 1] - a_cum_t[h:h + 1, :]
            decay = jnp.exp(jnp.where(tril, seg, neg_inf))
            wts = (cbm * decay * dt_t[h:h + 1, :]).astype(BF16)
            yd = jnp.dot(wts, xb, preferred_element_type=F32)
            sel = head_of_col == r
            y = jnp.where(sel, yd, y)
            ea_g = jnp.where(sel, ea[:, h:h + 1], ea_g)
            te_g = jnp.where(sel, to_end[:, h:h + 1], te_g)
            cd_g = jnp.where(sel[0:1, :], cdec[:, h:h + 1], cd_g)
        y = y + y_off * ea_g + dsk_ref[:, xs] * xg
        xs_scaled = (xg * te_g).astype(BF16)
        st = jnp.dot(bg.T.astype(BF16), xs_scaled, preferred_element_type=F32)
        state_ref[:, xs] = prev_t * cd_g + st

        yz = y * _silu(z_ref[0, :, xs])
        ms = jnp.mean(yz * yz, axis=-1, keepdims=True)
        o_ref[0, :, xs] = (yz * lax.rsqrt(ms + EPS) * nw_ref[:, xs]).astype(o_ref.dtype)


def _ssd(pf, dtraw, cw, cb, dtb, alog, dsk, nw):
    b, s, _ = pf.shape
    nc = s // CHUNK
    const2 = lambda bi, c: (0, 0)
    return pl.pallas_call(
        _ssd_kernel,
        grid=(b, nc),
        in_specs=[pl.BlockSpec((1, CHUNK, CONV_CH), lambda bi, c: (bi, c, 0)),
                  pl.BlockSpec((1, CHUNK, D_INNER), lambda bi, c: (bi, c, PF_Z_BLOCK)),
                  pl.BlockSpec((1, CHUNK, LANES), lambda bi, c: (bi, c, 0)),
                  pl.BlockSpec((CONV_K, CONV_CH), const2),
                  pl.BlockSpec((1, CONV_CH), const2),
                  pl.BlockSpec((1, LANES), const2),
                  pl.BlockSpec((1, LANES), const2),
                  pl.BlockSpec((1, D_INNER), const2),
                  pl.BlockSpec((1, D_INNER), const2)],
        out_specs=pl.BlockSpec((1, CHUNK, D_INNER), lambda bi, c: (bi, c, 0)),
        out_shape=jax.ShapeDtypeStruct((b, s, D_INNER), BF16),
        scratch_shapes=[pltpu.VMEM((SSD_STATE, D_INNER), F32),
                        pltpu.VMEM((CONV_HALO + CHUNK, CONV_CH), F32),
                        pltpu.VMEM((CHUNK, CONV_CH), F32)],
        compiler_params=_cparams(("parallel", "arbitrary")),
        name="ssd",
    )(pf, pf, dtraw, cw, cb, dtb, alog, dsk, nw)


ATT_TILE = 512


def _t5_bias_lut(rb_row):
    n = lax.broadcasted_iota(I32, (LANES, LANES), 1)
    bkt = lax.broadcasted_iota(I32, (LANES, LANES), 0)
    max_exact = NUM_BUCKETS // 2
    nf = jnp.maximum(n, 1).astype(F32)
    large = max_exact + (jnp.log(nf / max_exact) / math.log(MAX_DISTANCE / max_exact)
                         * (NUM_BUCKETS - max_exact)).astype(I32)
    large = jnp.minimum(large, NUM_BUCKETS - 1)
    bucket = jnp.where(n < max_exact, n, large)
    onehot = (bucket == bkt).astype(F32)
    lut = jnp.dot(jnp.broadcast_to(rb_row, (SUBLANES, LANES)), onehot,
                  preferred_element_type=F32, precision=HIGHEST)
    return lut[0:1, :]


def _attn_kernel(pmin_ref, pmax_ref, qt_ref, k_ref, vt_ref, pos_ref, rb_ref,
                 lamv_ref, nw_ref, o_ref, qm_ref, m_ref, l_ref, acc_ref, sa_ref, sb_ref, *, tile):
    T = tile
    bi = pl.program_id(0)
    qi = pl.program_id(2)
    neg_inf = jnp.float32(-jnp.inf)
    nck = T // LANES

    qt = qt_ref[0]
    feat = lax.broadcasted_iota(I32, (2 * ATTN_HEAD_DIM, T), 0)
    qs = qt * jnp.asarray(ATTN_HEAD_DIM ** -0.5, qt.dtype)
    zero = jnp.zeros_like(qs)
    qm_ref[0] = jnp.where(feat < ATTN_HEAD_DIM, qs, zero)
    qm_ref[1] = jnp.where(feat >= ATTN_HEAD_DIM, qs, zero)
    m_ref[...] = jnp.full_like(m_ref, neg_inf)
    l_ref[...] = jnp.zeros_like(l_ref)
    acc_ref[...] = jnp.zeros_like(acc_ref)

    lut = _t5_bias_lut(rb_ref[0])
    c_far = lut[:, LANES - 1:LANES]
    lut_b = jnp.broadcast_to(lut, (LANES, LANES))
    qoff = pl.multiple_of(qi * T, T)
    pos_q = pos_ref[0, :, pl.ds(qoff, T)]

    def scores(ki, dst_ref):
        koff = pl.multiple_of(ki * T, T)
        k = k_ref[0, pl.ds(koff, T), :]
        for i in range(2):
            dst_ref[i] = jnp.dot(k, qm_ref[i], preferred_element_type=F32)

    def consume(ki, src_ref, near, diag):
        koff = pl.multiple_of(ki * T, T)
        vt = vt_ref[0, :, pl.ds(koff, T)]
        if near:
            rows = []
            for kc in range(nck):
                pk_row = pos_ref[0, :, pl.ds(koff + kc * LANES, LANES)]
                pk = jnp.broadcast_to(pk_row, (LANES, LANES)).T
                blocks = []
                for qc in range(nck):
                    dist = jnp.clip(pos_q[:, qc * LANES:(qc + 1) * LANES] - pk, 0, LANES - 1)
                    blocks.append(jnp.take_along_axis(lut_b, dist, axis=1))
                rows.append(jnp.concatenate(blocks, axis=1))
            bias = jnp.concatenate(rows, axis=0)
            if diag:
                kr = lax.broadcasted_iota(I32, (T, T), 0)
                qc_ = lax.broadcasted_iota(I32, (T, T), 1)
                bias = jnp.where(kr <= qc_, bias, neg_inf)
        for i in range(2):
            s = src_ref[i]
            if near:
                s = s + bias
                m_tile = jnp.max(s, axis=0, keepdims=True)
            else:
                m_tile = jnp.max(s, axis=0, keepdims=True) + c_far
            m_prev = m_ref[i]
            m_new = jnp.maximum(m_prev, m_tile)
            shift = m_new if near else m_new - c_far
            p = jnp.exp(s - shift)
            alpha = jnp.exp(m_prev - m_new)
            l_ref[i] = alpha * l_ref[i] + jnp.sum(p, axis=0, keepdims=True)
            acc_ref[i] = alpha * acc_ref[i] + jnp.dot(vt, p.astype(vt.dtype),
                                                      preferred_element_type=F32)
            m_ref[i] = m_new

    def step(t, src_ref, dst_ref):
        near = pmin_ref[bi, qi] - pmax_ref[bi, t] < N_FAR

        @pl.when(near)
        def _():
            scores(t + 1, dst_ref)
            consume(t, src_ref, True, False)

        @pl.when(jnp.logical_not(near))
        def _():
            scores(t + 1, dst_ref)
            consume(t, src_ref, False, False)

    def body(jj, carry):
        step(2 * jj, sa_ref, sb_ref)

        @pl.when(2 * jj + 1 < qi)
        def _():
            step(2 * jj + 1, sb_ref, sa_ref)

        return carry

    scores(0, sa_ref)
    lax.fori_loop(0, (qi + 1) // 2, body, 0)

    @pl.when(qi % 2 == 0)
    def _():
        consume(qi, sa_ref, True, True)

    @pl.when(qi % 2 == 1)
    def _():
        consume(qi, sb_ref, True, True)

    lv = lamv_ref[...]
    lam = (jnp.exp(jnp.sum(lv[0:1, :] * lv[1:2, :], axis=-1, keepdims=True))
           - jnp.exp(jnp.sum(lv[2:3, :] * lv[3:4, :], axis=-1, keepdims=True))
           + LAMBDA_INIT)
    att = acc_ref[0] / l_ref[0] - lam * (acc_ref[1] / l_ref[1])
    ms = jnp.mean(att * att, axis=0, keepdims=True)
    y = att * lax.rsqrt(ms + EPS) * nw_ref[...] * (1.0 - LAMBDA_INIT)
    o_ref[0] = y.T.astype(o_ref.dtype)


def _attn(qvt, kk, positions, rel_bias, lamv, nw_col):
    b, s, _ = kk.shape
    T = min(ATT_TILE, s)
    nq = s // T
    H = ATTN_HEADS
    pos_blk = positions.reshape(b, nq, T)
    pmin = jnp.min(pos_blk, axis=-1)
    pmax = jnp.max(pos_blk, axis=-1)
    posr = positions.reshape(b, 1, s)
    rb = jnp.zeros((H, 1, LANES), F32).at[:, 0, :NUM_BUCKETS].set(rel_bias.T)
    grid_spec = pltpu.PrefetchScalarGridSpec(
        num_scalar_prefetch=2,
        grid=(b, H, nq),
        in_specs=[pl.BlockSpec((1, LANES, T), lambda bi, h, qi, *_: (bi, h, qi)),
                  pl.BlockSpec((1, s, LANES), lambda bi, h, qi, *_: (bi, 0, h)),
                  pl.BlockSpec((1, LANES, s), lambda bi, h, qi, *_: (bi, H + h, 0)),
                  pl.BlockSpec((1, 1, s), lambda bi, h, qi, *_: (bi, 0, 0)),
                  pl.BlockSpec((1, 1, LANES), lambda bi, h, qi, *_: (h, 0, 0)),
                  pl.BlockSpec((SUBLANES, LANES), lambda bi, h, qi, *_: (0, 0)),
                  pl.BlockSpec((LANES, 1), lambda bi, h, qi, *_: (0, 0))],
        out_specs=pl.BlockSpec((1, T, LANES), lambda bi, h, qi, *_: (bi, qi, h)),
        scratch_shapes=[pltpu.VMEM((2, LANES, T), BF16),
                        pltpu.VMEM((2, 1, T), F32),
                        pltpu.VMEM((2, 1, T), F32),
                        pltpu.VMEM((2, LANES, T), F32),
                        pltpu.VMEM((2, T, T), F32),
                        pltpu.VMEM((2, T, T), F32)])
    return pl.pallas_call(
        functools.partial(_attn_kernel, tile=T),
        grid_spec=grid_spec,
        out_shape=jax.ShapeDtypeStruct((b, s, H * ATTN_V_DIM), BF16),
        compiler_params=_cparams(("parallel", "parallel", "arbitrary")),
        name="attn",
    )(pmin, pmax, qvt, kk, qvt, posr, rb, lamv, nw_col)


def _merge_kernel(x_ref, ys_ref, ya_ref, gs_ref, ga_ref, g1_ref, ws_ref, wa_ref, wo_ref, o_ref):
    y_ssd = jnp.dot(ys_ref[0], ws_ref[...], preferred_element_type=F32)
    y_att = jnp.dot(ya_ref[0], wa_ref[...], preferred_element_type=F32)
    mix_in = _sigmoid(gs_ref[0]) * y_ssd + _sigmoid(ga_ref[0]) * y_att
    mix = jnp.dot(mix_in.astype(BF16), wo_ref[...], preferred_element_type=F32)
    o_ref[0] = x_ref[0] + g1_ref[0] * mix


def _merge(x, ys, ya, pf, g1, ws, wa, wo, tm=512):
    b, s, d = x.shape
    tm = min(tm, s)
    const2 = lambda bi, m: (0, 0)
    return pl.pallas_call(
        _merge_kernel,
        grid=(b, s // tm),
        in_specs=[pl.BlockSpec((1, tm, d), lambda bi, m: (bi, m, 0)),
                  pl.BlockSpec((1, tm, D_INNER), lambda bi, m: (bi, m, 0)),
                  pl.BlockSpec((1, tm, d), lambda bi, m: (bi, m, 0)),
                  pl.BlockSpec((1, tm, d), lambda bi, m: (bi, m, PF_GS_BLOCK)),
                  pl.BlockSpec((1, tm, d), lambda bi, m: (bi, m, PF_GA_BLOCK)),
                  pl.BlockSpec((1, 1, d), lambda bi, m: (bi, 0, 0)),
                  pl.BlockSpec((D_INNER, d), const2),
                  pl.BlockSpec((d, d), const2),
                  pl.BlockSpec((d, d), const2)],
        out_specs=pl.BlockSpec((1, tm, d), lambda bi, m: (bi, m, 0)),
        out_shape=jax.ShapeDtypeStruct((b, s, d), F32),
        compiler_params=_cparams(("parallel", "parallel")),
        name="merge",
    )(x, ys, ya, pf, pf, g1, ws, wa, wo)


TOPK_TOKENS = 256
N_INST = 2 * PEER_HEADS
PAIR_CANDS = [(a, c) for a in range(PEER_TOPK) for c in range(PEER_TOPK)
              if (a + 1) * (c + 1) <= PEER_TOPK]


def _topk_kernel(q_ref, kp_ref, e1_ref, e2_ref, g_ref, s_ref, v_ref, i_ref, cs_ref, ce_ref,
                 os_ref, oe_ref):
    tn = q_ref.shape[0]
    neg_inf = jnp.float32(-jnp.inf)
    s_ref[...] = _dot_nt(kp_ref[...], q_ref[...].astype(BF16))

    def level1(it, prev):
        best = jnp.full((N_INST, tn), neg_inf, F32)
        bidx = jnp.zeros((N_INST, tn), I32)
        for key in range(N_KEYS):
            rows = slice(key * N_INST, (key + 1) * N_INST)
            sk = jnp.where(prev == key, neg_inf, s_ref[rows, :])
            s_ref[rows, :] = sk
            upd = sk > best
            best = jnp.where(upd, sk, best)
            bidx = jnp.where(upd, key, bidx)
        v_ref[it] = best
        i_ref[it] = bidx
        return bidx

    lax.fori_loop(0, PEER_TOPK, level1, jnp.full((N_INST, tn), -1, I32))

    H = PEER_HEADS
    for ci, (a, c) in enumerate(PAIR_CANDS):
        cs_ref[ci] = v_ref[a, 0:H, :] + v_ref[c, H:2 * H, :]
        ce_ref[ci] = i_ref[a, 0:H, :] * N_KEYS + i_ref[c, H:2 * H, :]

    def level2(it, prev):
        best = jnp.full((H, tn), neg_inf, F32)
        bci = jnp.zeros((H, tn), I32)
        bex = jnp.zeros((H, tn), I32)
        for ci in range(len(PAIR_CANDS)):
            sk = jnp.where(prev == ci, neg_inf, cs_ref[ci])
            cs_ref[ci] = sk
            upd = sk > best
            best = jnp.where(upd, sk, best)
            bci = jnp.where(upd, ci, bci)
            bex = jnp.where(upd, ce_ref[ci], bex)
        os_ref[it] = best
        oe_ref[it] = bex
        return bci

    lax.fori_loop(0, PEER_TOPK, level2, jnp.full((H, tn), -1, I32))

    sc = os_ref[...]
    ex = jnp.exp(sc - sc[0:1])
    gate = ex / jnp.sum(ex, axis=0, keepdims=True)
    ex_i = oe_ref[...]
    rows = PEER_TOPK * H
    g_ref[...] = gate.reshape(rows, tn).T
    key_bits = N_KEYS.bit_length() - 1
    e1 = lax.shift_right_logical(ex_i, key_bits).astype(F32).reshape(rows, tn)
    e2 = lax.bitwise_and(ex_i, N_KEYS - 1).astype(F32).reshape(rows, tn)
    e1_ref[...] = e1.T.astype(I32)
    e2_ref[...] = e2.T.astype(I32)


def _peer_topk(q, kperm):
    t, d = q.shape
    tn = min(TOPK_TOKENS, t)
    lanes = PEER_TOPK * PEER_HEADS
    ncand = len(PAIR_CANDS)
    out = jax.ShapeDtypeStruct((t, lanes), I32)
    return pl.pallas_call(
        _topk_kernel,
        grid=(t // tn,),
        in_specs=[pl.BlockSpec((tn, d), lambda i: (i, 0)),
                  pl.BlockSpec(kperm.shape, lambda i: (0, 0))],
        out_specs=[pl.BlockSpec((tn, lanes), lambda i: (i, 0))] * 3,
        out_shape=[out, out, jax.ShapeDtypeStruct((t, lanes), F32)],
        scratch_shapes=[pltpu.VMEM((N_KEYS * N_INST, tn), F32),
                        pltpu.VMEM((PEER_TOPK, N_INST, tn), F32),
                        pltpu.VMEM((PEER_TOPK, N_INST, tn), I32),
                        pltpu.VMEM((ncand, PEER_HEADS, tn), F32),
                        pltpu.VMEM((ncand, PEER_HEADS, tn), I32),
                        pltpu.VMEM((PEER_TOPK, PEER_HEADS, tn), F32),
                        pltpu.VMEM((PEER_TOPK, PEER_HEADS, tn), I32)],
        compiler_params=_cparams(("parallel",)),
        name="peer_topk",
    )(q, kperm)


MIX_TOKENS = 256
MIX_EXPERTS = 1024
MIX_SUB = 256
G_PITCH = N_KEYS + SUBLANES
G_BUILD_UNROLL = 8


def _gelu_exact(x):
    return 0.5 * x * (1.0 + lax.erf(x * (1.0 / math.sqrt(2.0))))


def _mix_kernel(h_ref, nw_ref, sc_ref, sh_ref, g2_ref, fw_ref, e1_ref, e2_ref, gt_ref,
                u_ref, v_ref, o_ref, xb_ref, ghi_ref, glo_ref, g_ref, acc_ref):
    tb = h_ref.shape[1]
    c = pl.program_id(2)

    @pl.when(c == 0)
    def _():
        xn = _modulated_norm(h_ref[0], nw_ref[...], sc_ref[0], sh_ref[0])
        xb_ref[...] = xn.astype(BF16)
        acc_ref[...] = jnp.zeros_like(acc_ref)
        gate = gt_ref[...]
        ghi = gate.astype(BF16).astype(F32)
        ghi_ref[...] = ghi
        glo_ref[...] = gate - ghi
        sub = lax.broadcasted_iota(I32, (N_KEYS, LANES), 0)

        def build(t, carry):
            eq1 = sub == e1_ref[pl.ds(t, 1), :]
            eq2 = sub == e2_ref[pl.ds(t, 1), :]
            ghi_t = ghi_ref[pl.ds(t, 1), :]
            glo_t = glo_ref[pl.ds(t, 1), :]
            a = jnp.concatenate([jnp.where(eq1, ghi_t, 0.0), jnp.where(eq1, glo_t, 0.0)],
                                axis=1).astype(BF16)
            one = jnp.where(eq2, 1.0, 0.0)
            bm = jnp.concatenate([one, one], axis=1).astype(BF16)
            g_ref[pl.ds(pl.multiple_of(t * G_PITCH, SUBLANES), N_KEYS), :] = _dot_nt(a, bm)
            return carry

        lax.fori_loop(0, tb, build, 0, unroll=G_BUILD_UNROLL)

    xb = xb_ref[...]
    per_key = MIX_SUB // N_KEYS
    for j in range(MIX_EXPERTS // MIX_SUB):
        rows = slice(j * MIX_SUB, (j + 1) * MIX_SUB)
        hid = _dot_nt(xb, u_ref[rows, :])
        key0 = c * (MIX_EXPERTS // N_KEYS) + j * per_key
        gates = jnp.concatenate(
            [g_ref[pl.ds(key0 + i, tb, stride=G_PITCH), :] for i in range(per_key)], axis=1)
        p = (_gelu_exact(hid) * gates).astype(BF16)
        acc_ref[...] += jnp.dot(p, v_ref[rows, :], preferred_element_type=F32)

    @pl.when(c == pl.num_programs(2) - 1)
    def _():
        h2 = h_ref[0] + g2_ref[0] * acc_ref[...]
        r = h2 * lax.rsqrt(jnp.mean(h2 * h2, axis=-1, keepdims=True) + EPS)
        o_ref[0] = r * fw_ref[...]


def _peer_mix(h, nw, sc, sh, g2, fw, e1, e2, gate, u, v):
    b, s, d = h.shape
    tb = min(MIX_TOKENS, s)
    nb = s // tb
    ne = u.shape[0] // MIX_EXPERTS
    lanes = PEER_TOPK * PEER_HEADS
    tok = lambda bi, m, c: (bi * nb + m, 0)
    vec = lambda bi, m, c: (bi, 0, 0)
    return pl.pallas_call(
        _mix_kernel,
        grid=(b, nb, ne),
        in_specs=[pl.BlockSpec((1, tb, d), lambda bi, m, c: (bi, m, 0)),
                  pl.BlockSpec((1, d), lambda bi, m, c: (0, 0)),
                  pl.BlockSpec((1, 1, d), vec),
                  pl.BlockSpec((1, 1, d), vec),
                  pl.BlockSpec((1, 1, d), vec),
                  pl.BlockSpec((1, d), lambda bi, m, c: (0, 0)),
                  pl.BlockSpec((tb, lanes), tok),
                  pl.BlockSpec((tb, lanes), tok),
                  pl.BlockSpec((tb, lanes), tok),
                  pl.BlockSpec((MIX_EXPERTS, d), lambda bi, m, c: (c, 0)),
                  pl.BlockSpec((MIX_EXPERTS, d), lambda bi, m, c: (c, 0))],
        out_specs=pl.BlockSpec((1, tb, d), lambda bi, m, c: (bi, m, 0)),
        out_shape=jax.ShapeDtypeStruct((b, s, d), F32),
        scratch_shapes=[pltpu.VMEM((tb, d), BF16),
                        pltpu.VMEM((tb, lanes), F32),
                        pltpu.VMEM((tb, lanes), F32),
                        pltpu.VMEM((tb * G_PITCH, LANES), F32),
                        pltpu.VMEM((tb, d), F32)],
        compiler_params=_cparams(("parallel", "parallel", "arbitrary")),
        name="peer_mix",
    )(h, nw, sc, sh, g2, fw, e1, e2, gate, u, v)


def _pad_lanes(vec, width=LANES):
    return jnp.zeros((1, width), F32).at[0, :vec.shape[0]].set(vec)


def kernel(x, c, positions, ada_w, ada_b, norm1_w, norm2_w, final_norm_w, w_in, conv_w, conv_b,
           dt_bias, a_log, d_skip, ssd_norm_w, w_ssd_o, lambda_q1, lambda_k1, lambda_q2,
           lambda_k2, attn_norm_w, w_attn_o, rel_bias, w_out, peer_wq, peer_key1, peer_key2,
           peer_u, peer_v):
    b, s, d = x.shape
    l = 0

    c8 = jnp.zeros((SUBLANES, d), F32).at[:b].set(c)
    ada = _ada(c8, ada_w[l], ada_b[l][None, :])[:b]
    sh1, sc1, g1, sh2, sc2, g2 = [m[:, None, :] for m in jnp.split(ada, 6, axis=-1)]

    wl = w_in[l]
    w_f = jnp.concatenate([wl[:, OFF_XBC:OFF_DT], wl[:, OFF_Z:OFF_XBC], wl[:, OFF_GS:IN_COLS]],
                          axis=1).astype(BF16)
    w_k = wl[:, OFF_K:OFF_V].astype(BF16)
    w_qv_t = jnp.concatenate([wl[:, OFF_Q:OFF_K], wl[:, OFF_V:OFF_GS]], axis=1).T.astype(BF16)
    w_dt = jnp.zeros((d, LANES), F32).at[:, :SSD_HEADS].set(wl[:, OFF_DT:OFF_Q])
    nw1 = norm1_w[l][None, :]

    pf = _norm_proj(x, nw1, sc1, sh1, w_f, F32, 512, 1024, "proj_f32")
    kk = _norm_proj(x, nw1, sc1, sh1, w_k, BF16, 512, 1024, "proj_k")
    qvt = _norm_proj_t(x, nw1, sc1, sh1, w_qv_t, BF16, 512, 1024, "proj_qvt")
    dtraw = _norm_proj(x, nw1, sc1, sh1, w_dt, F32, 512, LANES, "proj_dt", precision=HIGHEST)

    dsk = jnp.repeat(d_skip[l], SSD_HEAD_DIM)[None, :]
    yn = _ssd(pf, dtraw, conv_w[l], conv_b[l][None, :], _pad_lanes(dt_bias[l]),
              _pad_lanes(a_log[l]), dsk, ssd_norm_w[l][None, :])

    lamv = jnp.zeros((SUBLANES, LANES), F32)
    for i, v_ in enumerate((lambda_q1, lambda_k1, lambda_q2, lambda_k2)):
        lamv = lamv.at[i, :ATTN_HEAD_DIM].set(v_[l])
    an = _attn(qvt, kk, positions, rel_bias, lamv, attn_norm_w[l][:, None])

    h1 = _merge(x, yn, an, pf, g1, w_ssd_o[l].astype(BF16), w_attn_o[l].astype(BF16),
                w_out[l].astype(BF16))

    nw2 = norm2_w[l][None, :]
    q = _norm_proj(h1, nw2, sc2, sh2, peer_wq[l].astype(BF16), F32, 512, 1024, "peer_q")
    keys = jnp.stack([peer_key1[l], peer_key2[l]])
    kperm = jnp.einsum('ahkd,ab,hg->kahgbd', keys, jnp.eye(2, dtype=F32),
                       jnp.eye(PEER_HEADS, dtype=F32))
    kperm = kperm.reshape(N_KEYS * N_INST, PEER_HEADS * PEER_DK).astype(BF16)
    e1, e2, gate = _peer_topk(q.reshape(b * s, PEER_HEADS * PEER_DK), kperm)
    return _peer_mix(h1, nw2, sc2, sh2, g2, final_norm_w[None, :], e1, e2, gate,
                     peer_u[l].astype(BF16), peer_v[l].astype(BF16))
```

```python
import functools
import math

import jax
import jax.numpy as jnp
from jax import lax
from jax.experimental import pallas as pl
from jax.experimental.pallas import tpu as pltpu

F32 = jnp.float32
BF16 = jnp.bfloat16
I32 = jnp.int32
HIGHEST = lax.Precision.HIGHEST

D_MODEL = 1024
D_INNER = 2048
SSD_HEADS = 32
SSD_HEAD_DIM = 64
SSD_GROUPS = 8
SSD_STATE = 128
CONV_K = 4
CONV_CH = D_INNER + 2 * SSD_GROUPS * SSD_STATE
CHUNK = 128
ATTN_HEADS = 8
ATTN_HEAD_DIM = 64
ATTN_V_DIM = 2 * ATTN_HEAD_DIM
NUM_BUCKETS = 32
MAX_DISTANCE = 128
PEER_HEADS = 8
N_KEYS = 128
N_EXPERTS = N_KEYS * N_KEYS
PEER_DK = 128
PEER_TOPK = 16
EPS = 1e-6
LAMBDA_INIT = 0.8 - 0.6 * math.exp(0.0)

LANES = 128
SUBLANES = 8
VMEM_LIMIT = 48 * 1024 * 1024

OFF_Z = 0
OFF_XBC = OFF_Z + D_INNER
OFF_DT = OFF_XBC + CONV_CH
OFF_Q = OFF_DT + SSD_HEADS
OFF_K = OFF_Q + ATTN_HEADS * 2 * ATTN_HEAD_DIM
OFF_V = OFF_K + ATTN_HEADS * 2 * ATTN_HEAD_DIM
OFF_GS = OFF_V + ATTN_HEADS * ATTN_V_DIM
OFF_GA = OFF_GS + D_MODEL
IN_COLS = OFF_GA + D_MODEL


def _t5_far_distance():
    max_exact = NUM_BUCKETS // 2
    n = max_exact
    while True:
        b = max_exact + int(math.log(n / max_exact) / math.log(MAX_DISTANCE / max_exact)
                            * (NUM_BUCKETS - max_exact))
        if b >= NUM_BUCKETS - 1:
            return n
        n += 1


N_FAR = _t5_far_distance()
assert N_FAR <= LANES


def _sigmoid(x):
    return 1.0 / (1.0 + jnp.exp(-x))


def _silu(x):
    return x * _sigmoid(x)


def _cparams(sem):
    return pltpu.CompilerParams(dimension_semantics=sem, vmem_limit_bytes=VMEM_LIMIT)


def _dot_nt(a, b):
    return lax.dot_general(a, b, (((1,), (1,)), ((), ())), preferred_element_type=F32)


def _ada_kernel(c_ref, w_ref, b_ref, o_ref):
    o_ref[...] = jnp.dot(_silu(c_ref[...]), w_ref[...], preferred_element_type=F32,
                         precision=HIGHEST) + b_ref[...]


def _ada(c8, w, b):
    n = w.shape[1]
    tn = D_MODEL
    return pl.pallas_call(
        _ada_kernel,
        grid=(n // tn,),
        in_specs=[pl.BlockSpec((SUBLANES, D_MODEL), lambda j: (0, 0)),
                  pl.BlockSpec((D_MODEL, tn), lambda j: (0, j)),
                  pl.BlockSpec((1, tn), lambda j: (0, j))],
        out_specs=pl.BlockSpec((SUBLANES, tn), lambda j: (0, j)),
        out_shape=jax.ShapeDtypeStruct((SUBLANES, n), F32),
        compiler_params=_cparams(("arbitrary",)),
        name="ada",
    )(c8, w, b)


def _modulated_norm(x, nw, sc, sh):
    r = x * lax.rsqrt(jnp.mean(x * x, axis=-1, keepdims=True) + EPS) * nw
    return r * (1.0 + sc) + sh


def _norm_proj_kernel(x_ref, nw_ref, sc_ref, sh_ref, w_ref, o_ref, xn_ref, *, precision):
    @pl.when(pl.program_id(2) == 0)
    def _():
        xn = _modulated_norm(x_ref[0], nw_ref[...], sc_ref[0], sh_ref[0])
        xn_ref[...] = xn.astype(xn_ref.dtype)

    o_ref[0] = jnp.dot(xn_ref[...], w_ref[...], preferred_element_type=F32,
                       precision=precision).astype(o_ref.dtype)


def _norm_proj(x, nw, sc, sh, w, out_dtype, tm, tn, name, precision=None):
    b, s, d = x.shape
    n = w.shape[1]
    tm = min(tm, s)
    tn = min(tn, n)
    return pl.pallas_call(
        functools.partial(_norm_proj_kernel, precision=precision),
        grid=(b, s // tm, n // tn),
        in_specs=[pl.BlockSpec((1, tm, d), lambda bi, m, j: (bi, m, 0)),
                  pl.BlockSpec((1, d), lambda bi, m, j: (0, 0)),
                  pl.BlockSpec((1, 1, d), lambda bi, m, j: (bi, 0, 0)),
                  pl.BlockSpec((1, 1, d), lambda bi, m, j: (bi, 0, 0)),
                  pl.BlockSpec((d, tn), lambda bi, m, j: (0, j))],
        out_specs=pl.BlockSpec((1, tm, tn), lambda bi, m, j: (bi, m, j)),
        out_shape=jax.ShapeDtypeStruct((b, s, n), out_dtype),
        scratch_shapes=[pltpu.VMEM((tm, d), w.dtype)],
        compiler_params=_cparams(("parallel", "parallel", "arbitrary")),
        name=name,
    )(x, nw, sc, sh, w)


def _norm_proj_t_kernel(x_ref, nw_ref, sc_ref, sh_ref, wt_ref, rs_ref, o_ref, xn_ref):
    @pl.when(pl.program_id(2) == 0)
    def _():
        xn = _modulated_norm(x_ref[0], nw_ref[...], sc_ref[0], sh_ref[0])
        xn_ref[...] = xn.astype(xn_ref.dtype)

    o_ref[0] = (_dot_nt(wt_ref[...], xn_ref[...]) * rs_ref[...]).astype(o_ref.dtype)


def _norm_proj_t(x, nw, sc, sh, wt, row_scale, out_dtype, tm, tn, name):
    b, s, d = x.shape
    n = wt.shape[0]
    tm = min(tm, s)
    tn = min(tn, n)
    return pl.pallas_call(
        _norm_proj_t_kernel,
        grid=(b, s // tm, n // tn),
        in_specs=[pl.BlockSpec((1, tm, d), lambda bi, m, j: (bi, m, 0)),
                  pl.BlockSpec((1, d), lambda bi, m, j: (0, 0)),
                  pl.BlockSpec((1, 1, d), lambda bi, m, j: (bi, 0, 0)),
                  pl.BlockSpec((1, 1, d), lambda bi, m, j: (bi, 0, 0)),
                  pl.BlockSpec((tn, d), lambda bi, m, j: (j, 0)),
                  pl.BlockSpec((tn, 1), lambda bi, m, j: (j, 0))],
        out_specs=pl.BlockSpec((1, tn, tm), lambda bi, m, j: (bi, j, m)),
        out_shape=jax.ShapeDtypeStruct((b, n, s), out_dtype),
        scratch_shapes=[pltpu.VMEM((tm, d), wt.dtype)],
        compiler_params=_cparams(("parallel", "parallel", "arbitrary")),
        name=name,
    )(x, nw, sc, sh, wt, row_scale)


CONV_HALO = SUBLANES
CONV_COLS = 512
GROUP_COLS = D_INNER // SSD_GROUPS
HEADS_PER_GROUP = SSD_HEADS // SSD_GROUPS
PF_COLS = CONV_CH + D_INNER + 2 * D_MODEL
PF_Z_BLOCK = CONV_CH // D_INNER
PF_GS_BLOCK = (CONV_CH + D_INNER) // D_MODEL
PF_GA_BLOCK = PF_GS_BLOCK + 1


def _softplus(x):
    return jnp.maximum(x, 0.0) + jnp.log(1.0 + jnp.exp(-jnp.abs(x)))


def _ssd_kernel(xbc_ref, z_ref, dt_ref, cw_ref, cb_ref, dtb_ref, alog_ref, dsk_ref, nw_ref,
                o_ref, state_ref, win_ref, act_ref):
    L = CHUNK

    @pl.when(pl.program_id(1) == 0)
    def _():
        state_ref[...] = jnp.zeros_like(state_ref)
        win_ref[0:CONV_HALO, :] = jnp.zeros((CONV_HALO, CONV_CH), F32)

    win_ref[CONV_HALO:CONV_HALO + L, :] = xbc_ref[0]
    base = CONV_HALO - (CONV_K - 1)
    for j in range(CONV_CH // CONV_COLS):
        cs = slice(j * CONV_COLS, (j + 1) * CONV_COLS)
        acc = cb_ref[:, cs] + cw_ref[0:1, cs] * win_ref[base:base + L, cs]
        for k in range(1, CONV_K):
            acc = acc + cw_ref[k:k + 1, cs] * win_ref[base + k:base + k + L, cs]
        act_ref[:, cs] = _silu(acc)
    win_ref[0:CONV_HALO, :] = win_ref[L:L + CONV_HALO, :]

    dtv = _softplus(dt_ref[0] + dtb_ref[...])
    a = -jnp.exp(alog_ref[...])
    row = lax.broadcasted_iota(I32, (L, L), 0)
    col = lax.broadcasted_iota(I32, (L, L), 1)
    tril = row >= col
    a_cum = jnp.dot(tril.astype(F32), dtv * a, preferred_element_type=F32,
                    precision=HIGHEST)
    a_cum_t = a_cum.T
    dt_t = dtv.T
    a_last = a_cum[L - 1:L, :]
    to_end = jnp.exp(a_last - a_cum) * dtv
    ea = jnp.exp(a_cum)
    cdec = jnp.exp(a_last)

    head_of_col = lax.broadcasted_iota(I32, (L, GROUP_COLS), 1) // SSD_HEAD_DIM
    neg_inf = jnp.float32(-jnp.inf)

    for g in range(SSD_GROUPS):
        xs = slice(g * GROUP_COLS, (g + 1) * GROUP_COLS)
        bs = slice(D_INNER + g * SSD_STATE, D_INNER + (g + 1) * SSD_STATE)
        cs = slice(D_INNER + (SSD_GROUPS + g) * SSD_STATE,
                   D_INNER + (SSD_GROUPS + g + 1) * SSD_STATE)
        xg = act_ref[:, xs]
        bg = act_ref[:, bs]
        xb = xg.astype(BF16)
        bb = bg.astype(BF16)
        cb16 = act_ref[:, cs].astype(BF16)
        cbm = _dot_nt(cb16, bb)
        prev_t = state_ref[:, xs]
        y_off = jnp.dot(cb16, prev_t.astype(BF16), preferred_element_type=F32)

        y = jnp.zeros((L, GROUP_COLS), F32)
        ea_g = jnp.zeros((L, GROUP_COLS), F32)
        te_g = jnp.zeros((L, GROUP_COLS), F32)
        cd_g = jnp.zeros((1, GROUP_COLS), F32)
        for r in range(HEADS_PER_GROUP):
            h = g * HEADS_PER_GROUP + r
            seg = a_cum[:, h:h + 1] - a_cum_t[h:h + 1, :]
            decay = jnp.exp(jnp.where(tril, seg, neg_inf))
            wts = (cbm * decay * dt_t[h:h + 1, :]).astype(BF16)
            yd = jnp.dot(wts, xb, preferred_element_type=F32)
            sel = head_of_col == r
            y = jnp.where(sel, yd, y)
            ea_g = jnp.where(sel, ea[:, h:h + 1], ea_g)
            te_g = jnp.where(sel, to_end[:, h:h + 1], te_g)
            cd_g = jnp.where(sel[0:1, :], cdec[:, h:h + 1], cd_g)
        y = y + y_off * ea_g + dsk_ref[:, xs] * xg
        xs_scaled = (xg * te_g).astype(BF16)
        st = jnp.dot(bg.T.astype(BF16), xs_scaled, preferred_element_type=F32)
        state_ref[:, xs] = prev_t * cd_g + st

        yz = y * _silu(z_ref[0, :, xs])
        ms = jnp.mean(yz * yz, axis=-1, keepdims=True)
        o_ref[0, :, xs] = (yz * lax.rsqrt(ms + EPS) * nw_ref[:, xs]).astype(o_ref.dtype)


def _ssd(pf, dtraw, cw, cb, dtb, alog, dsk, nw):
    b, s, _ = pf.shape
    nc = s // CHUNK
    const2 = lambda bi, c: (0, 0)
    return pl.pallas_call(
        _ssd_kernel,
        grid=(b, nc),
        in_specs=[pl.BlockSpec((1, CHUNK, CONV_CH), lambda bi, c: (bi, c, 0)),
                  pl.BlockSpec((1, CHUNK, D_INNER), lambda bi, c: (bi, c, PF_Z_BLOCK)),
                  pl.BlockSpec((1, CHUNK, LANES), lambda bi, c: (bi, c, 0)),
                  pl.BlockSpec((CONV_K, CONV_CH), const2),
                  pl.BlockSpec((1, CONV_CH), const2),
                  pl.BlockSpec((1, LANES), const2),
                  pl.BlockSpec((1, LANES), const2),
                  pl.BlockSpec((1, D_INNER), const2),
                  pl.BlockSpec((1, D_INNER), const2)],
        out_specs=pl.BlockSpec((1, CHUNK, D_INNER), lambda bi, c: (bi, c, 0)),
        out_shape=jax.ShapeDtypeStruct((b, s, D_INNER), BF16),
        scratch_shapes=[pltpu.VMEM((SSD_STATE, D_INNER), F32),
                        pltpu.VMEM((CONV_HALO + CHUNK, CONV_CH), F32),
                        pltpu.VMEM((CHUNK, CONV_CH), F32)],
        compiler_params=_cparams(("parallel", "arbitrary")),
        name="ssd",
    )(pf, pf, dtraw, cw, cb, dtb, alog, dsk, nw)


ATT_TILE = 512
LOG2E = math.log2(math.e)
BF16_ROWS = 2 * SUBLANES
ACC_ROWS = ATTN_V_DIM + BF16_ROWS


def _t5_bias_lut(rb_row):
    n = lax.broadcasted_iota(I32, (LANES, LANES), 1)
    bkt = lax.broadcasted_iota(I32, (LANES, LANES), 0)
    max_exact = NUM_BUCKETS // 2
    nf = jnp.maximum(n, 1).astype(F32)
    large = max_exact + (jnp.log(nf / max_exact) / math.log(MAX_DISTANCE / max_exact)
                         * (NUM_BUCKETS - max_exact)).astype(I32)
    large = jnp.minimum(large, NUM_BUCKETS - 1)
    bucket = jnp.where(n < max_exact, n, large)
    onehot = (bucket == bkt).astype(F32)
    lut = jnp.dot(jnp.broadcast_to(rb_row, (SUBLANES, LANES)), onehot,
                  preferred_element_type=F32, precision=HIGHEST)
    return lut[0:1, :]


def _attn_kernel(pmin_ref, pmax_ref, qt_ref, k_ref, vt_ref, pos_ref, rb_ref,
                 lamv_ref, nw_ref, o_ref, qm_ref, m_ref, acc_ref, sa_ref, sb_ref, *, tile):
    T = tile
    bi = pl.program_id(0)
    qi = pl.program_id(2)
    neg_inf = jnp.float32(-jnp.inf)
    nck = T // LANES

    qt = qt_ref[0]
    feat = lax.broadcasted_iota(I32, (2 * ATTN_HEAD_DIM, T), 0)
    zero = jnp.zeros_like(qt)
    qm_ref[0] = jnp.where(feat < ATTN_HEAD_DIM, qt, zero)
    qm_ref[1] = jnp.where(feat >= ATTN_HEAD_DIM, qt, zero)
    m_ref[...] = jnp.full_like(m_ref, neg_inf)
    acc_ref[...] = jnp.zeros_like(acc_ref)
    ones_rows = (lax.broadcasted_iota(I32, (BF16_ROWS, T), 0) == 0).astype(BF16)

    lut = _t5_bias_lut(rb_ref[0]) * LOG2E
    c_far = lut[:, LANES - 1:LANES]
    lut_b = jnp.broadcast_to(lut, (LANES, LANES))
    qoff = pl.multiple_of(qi * T, T)
    pos_q = pos_ref[0, :, pl.ds(qoff, T)]

    def scores(ki, dst_ref):
        koff = pl.multiple_of(ki * T, T)
        k = k_ref[0, pl.ds(koff, T), :]
        for i in range(2):
            dst_ref[i] = jnp.dot(k, qm_ref[i], preferred_element_type=F32)

    def consume(ki, src_ref, near, diag):
        koff = pl.multiple_of(ki * T, T)
        vt = jnp.concatenate([vt_ref[0, :, pl.ds(koff, T)], ones_rows], axis=0)
        if near:
            rows = []
            for kc in range(nck):
                pk_row = pos_ref[0, :, pl.ds(koff + kc * LANES, LANES)]
                pk = jnp.broadcast_to(pk_row, (LANES, LANES)).T
                blocks = []
                for qc in range(nck):
                    dist = jnp.clip(pos_q[:, qc * LANES:(qc + 1) * LANES] - pk, 0, LANES - 1)
                    blocks.append(jnp.take_along_axis(lut_b, dist, axis=1))
                rows.append(jnp.concatenate(blocks, axis=1))
            bias = jnp.concatenate(rows, axis=0)
            if diag:
                kr = lax.broadcasted_iota(I32, (T, T), 0)
                qc_ = lax.broadcasted_iota(I32, (T, T), 1)
                bias = jnp.where(kr <= qc_, bias, neg_inf)
        for i in range(2):
            s = src_ref[i]
            if near:
                s = s + bias
                m_tile = jnp.max(s, axis=0, keepdims=True)
            else:
                m_tile = jnp.max(s, axis=0, keepdims=True) + c_far
            m_prev = m_ref[i]
            m_new = jnp.maximum(m_prev, m_tile)
            shift = m_new if near else m_new - c_far
            p = jnp.exp2(s - shift)
            alpha = jnp.exp2(m_prev - m_new)
            acc_ref[i] = alpha * acc_ref[i] + jnp.dot(vt, p.astype(vt.dtype),
                                                      preferred_element_type=F32)
            m_ref[i] = m_new

    def step(t, src_ref, dst_ref):
        near = pmin_ref[bi, qi] - pmax_ref[bi, t] < N_FAR

        @pl.when(near)
        def _():
            scores(t + 1, dst_ref)
            consume(t, src_ref, True, False)

        @pl.when(jnp.logical_not(near))
        def _():
            scores(t + 1, dst_ref)
            consume(t, src_ref, False, False)

    def body(jj, carry):
        step(2 * jj, sa_ref, sb_ref)

        @pl.when(2 * jj + 1 < qi)
        def _():
            step(2 * jj + 1, sb_ref, sa_ref)

        return carry

    scores(0, sa_ref)
    lax.fori_loop(0, (qi + 1) // 2, body, 0)

    @pl.when(qi % 2 == 0)
    def _():
        consume(qi, sa_ref, True, True)

    @pl.when(qi % 2 == 1)
    def _():
        consume(qi, sb_ref, True, True)

    lv = lamv_ref[...]
    lam = (jnp.exp(jnp.sum(lv[0:1, :] * lv[1:2, :], axis=-1, keepdims=True))
           - jnp.exp(jnp.sum(lv[2:3, :] * lv[3:4, :], axis=-1, keepdims=True))
           + LAMBDA_INIT)
    nv = ATTN_V_DIM
    att = (acc_ref[0, 0:nv, :] / acc_ref[0, nv:nv + 1, :]
           - lam * (acc_ref[1, 0:nv, :] / acc_ref[1, nv:nv + 1, :]))
    ms = jnp.mean(att * att, axis=0, keepdims=True)
    y = att * lax.rsqrt(ms + EPS) * nw_ref[...] * (1.0 - LAMBDA_INIT)
    o_ref[0] = y.T.astype(o_ref.dtype)


def _attn(qvt, kk, positions, rel_bias, lamv, nw_col):
    b, s, _ = kk.shape
    T = min(ATT_TILE, s)
    nq = s // T
    H = ATTN_HEADS
    pos_blk = positions.reshape(b, nq, T)
    pmin = jnp.min(pos_blk, axis=-1)
    pmax = jnp.max(pos_blk, axis=-1)
    posr = positions.reshape(b, 1, s)
    rb = jnp.zeros((H, 1, LANES), F32).at[:, 0, :NUM_BUCKETS].set(rel_bias.T)
    grid_spec = pltpu.PrefetchScalarGridSpec(
        num_scalar_prefetch=2,
        grid=(b, H, nq),
        in_specs=[pl.BlockSpec((1, LANES, T), lambda bi, h, qi, *_: (bi, h, qi)),
                  pl.BlockSpec((1, s, LANES), lambda bi, h, qi, *_: (bi, 0, h)),
                  pl.BlockSpec((1, LANES, s), lambda bi, h, qi, *_: (bi, H + h, 0)),
                  pl.BlockSpec((1, 1, s), lambda bi, h, qi, *_: (bi, 0, 0)),
                  pl.BlockSpec((1, 1, LANES), lambda bi, h, qi, *_: (h, 0, 0)),
                  pl.BlockSpec((SUBLANES, LANES), lambda bi, h, qi, *_: (0, 0)),
                  pl.BlockSpec((LANES, 1), lambda bi, h, qi, *_: (0, 0))],
        out_specs=pl.BlockSpec((1, T, LANES), lambda bi, h, qi, *_: (bi, qi, h)),
        scratch_shapes=[pltpu.VMEM((2, LANES, T), BF16),
                        pltpu.VMEM((2, 1, T), F32),
                        pltpu.VMEM((2, ACC_ROWS, T), F32),
                        pltpu.VMEM((2, T, T), F32),
                        pltpu.VMEM((2, T, T), F32)])
    return pl.pallas_call(
        functools.partial(_attn_kernel, tile=T),
        grid_spec=grid_spec,
        out_shape=jax.ShapeDtypeStruct((b, s, H * ATTN_V_DIM), BF16),
        compiler_params=_cparams(("parallel", "parallel", "arbitrary")),
        name="attn",
    )(pmin, pmax, qvt, kk, qvt, posr, rb, lamv, nw_col)


def _merge_kernel(x_ref, ys_ref, ya_ref, gs_ref, ga_ref, g1_ref, ws_ref, wa_ref, wo_ref, o_ref):
    y_ssd = jnp.dot(ys_ref[0], ws_ref[...], preferred_element_type=F32)
    y_att = jnp.dot(ya_ref[0], wa_ref[...], preferred_element_type=F32)
    mix_in = _sigmoid(gs_ref[0]) * y_ssd + _sigmoid(ga_ref[0]) * y_att
    mix = jnp.dot(mix_in.astype(BF16), wo_ref[...], preferred_element_type=F32)
    o_ref[0] = x_ref[0] + g1_ref[0] * mix


def _merge(x, ys, ya, pf, g1, ws, wa, wo, tm=512):
    b, s, d = x.shape
    tm = min(tm, s)
    const2 = lambda bi, m: (0, 0)
    return pl.pallas_call(
        _merge_kernel,
        grid=(b, s // tm),
        in_specs=[pl.BlockSpec((1, tm, d), lambda bi, m: (bi, m, 0)),
                  pl.BlockSpec((1, tm, D_INNER), lambda bi, m: (bi, m, 0)),
                  pl.BlockSpec((1, tm, d), lambda bi, m: (bi, m, 0)),
                  pl.BlockSpec((1, tm, d), lambda bi, m: (bi, m, PF_GS_BLOCK)),
                  pl.BlockSpec((1, tm, d), lambda bi, m: (bi, m, PF_GA_BLOCK)),
                  pl.BlockSpec((1, 1, d), lambda bi, m: (bi, 0, 0)),
                  pl.BlockSpec((D_INNER, d), const2),
                  pl.BlockSpec((d, d), const2),
                  pl.BlockSpec((d, d), const2)],
        out_specs=pl.BlockSpec((1, tm, d), lambda bi, m: (bi, m, 0)),
        out_shape=jax.ShapeDtypeStruct((b, s, d), F32),
        compiler_params=_cparams(("parallel", "parallel")),
        name="merge",
    )(x, ys, ya, pf, pf, g1, ws, wa, wo)


TOPK_TOKENS = 256
N_INST = 2 * PEER_HEADS
PAIR_CANDS = [(a, c) for a in range(PEER_TOPK) for c in range(PEER_TOPK)
              if (a + 1) * (c + 1) <= PEER_TOPK]


def _topk_kernel(q_ref, kp_ref, e1_ref, e2_ref, g_ref, s_ref, v_ref, i_ref, cs_ref, ce_ref,
                 os_ref, oe_ref):
    tn = q_ref.shape[0]
    neg_inf = jnp.float32(-jnp.inf)
    s_ref[...] = _dot_nt(kp_ref[...], q_ref[...].astype(BF16))

    def level1(it, prev):
        best = jnp.full((N_INST, tn), neg_inf, F32)
        bidx = jnp.zeros((N_INST, tn), I32)
        for key in range(N_KEYS):
            rows = slice(key * N_INST, (key + 1) * N_INST)
            sk = jnp.where(prev == key, neg_inf, s_ref[rows, :])
            s_ref[rows, :] = sk
            upd = sk > best
            best = jnp.where(upd, sk, best)
            bidx = jnp.where(upd, key, bidx)
        v_ref[it] = best
        i_ref[it] = bidx
        return bidx

    lax.fori_loop(0, PEER_TOPK, level1, jnp.full((N_INST, tn), -1, I32))

    H = PEER_HEADS
    for ci, (a, c) in enumerate(PAIR_CANDS):
        cs_ref[ci] = v_ref[a, 0:H, :] + v_ref[c, H:2 * H, :]
        ce_ref[ci] = i_ref[a, 0:H, :] * N_KEYS + i_ref[c, H:2 * H, :]

    def level2(it, prev):
        best = jnp.full((H, tn), neg_inf, F32)
        bci = jnp.zeros((H, tn), I32)
        bex = jnp.zeros((H, tn), I32)
        for ci in range(len(PAIR_CANDS)):
            sk = jnp.where(prev == ci, neg_inf, cs_ref[ci])
            cs_ref[ci] = sk
            upd = sk > best
            best = jnp.where(upd, sk, best)
            bci = jnp.where(upd, ci, bci)
            bex = jnp.where(upd, ce_ref[ci], bex)
        os_ref[it] = best
        oe_ref[it] = bex
        return bci

    lax.fori_loop(0, PEER_TOPK, level2, jnp.full((H, tn), -1, I32))

    sc = os_ref[...]
    ex = jnp.exp(sc - sc[0:1])
    gate = ex / jnp.sum(ex, axis=0, keepdims=True)
    ex_i = oe_ref[...]
    rows = PEER_TOPK * H
    g_ref[...] = gate.reshape(rows, tn).T
    key_bits = N_KEYS.bit_length() - 1
    e1 = lax.shift_right_logical(ex_i, key_bits).astype(F32).reshape(rows, tn)
    e2 = lax.bitwise_and(ex_i, N_KEYS - 1).astype(F32).reshape(rows, tn)
    e1_ref[...] = e1.T.astype(I32)
    e2_ref[...] = e2.T.astype(I32)


def _peer_topk(q, kperm):
    t, d = q.shape
    tn = min(TOPK_TOKENS, t)
    lanes = PEER_TOPK * PEER_HEADS
    ncand = len(PAIR_CANDS)
    out = jax.ShapeDtypeStruct((t, lanes), I32)
    return pl.pallas_call(
        _topk_kernel,
        grid=(t // tn,),
        in_specs=[pl.BlockSpec((tn, d), lambda i: (i, 0)),
                  pl.BlockSpec(kperm.shape, lambda i: (0, 0))],
        out_specs=[pl.BlockSpec((tn, lanes), lambda i: (i, 0))] * 3,
        out_shape=[out, out, jax.ShapeDtypeStruct((t, lanes), F32)],
        scratch_shapes=[pltpu.VMEM((N_KEYS * N_INST, tn), F32),
                        pltpu.VMEM((PEER_TOPK, N_INST, tn), F32),
                        pltpu.VMEM((PEER_TOPK, N_INST, tn), I32),
                        pltpu.VMEM((ncand, PEER_HEADS, tn), F32),
                        pltpu.VMEM((ncand, PEER_HEADS, tn), I32),
                        pltpu.VMEM((PEER_TOPK, PEER_HEADS, tn), F32),
                        pltpu.VMEM((PEER_TOPK, PEER_HEADS, tn), I32)],
        compiler_params=_cparams(("parallel",)),
        name="peer_topk",
    )(q, kperm)


MIX_TOKENS = 256
MIX_EXPERTS = 1024
MIX_SUB = 256
G_PITCH = N_KEYS + SUBLANES
G_BUILD_UNROLL = 16


def _gelu_exact(x):
    return 0.5 * x * (1.0 + lax.erf(x * (1.0 / math.sqrt(2.0))))


def _mix_kernel(h_ref, nw_ref, sc_ref, sh_ref, g2_ref, fw_ref, e1_ref, e2_ref, gt_ref,
                ut_ref, v_ref, o_ref, xb_ref, ghi_ref, glo_ref, g_ref, acc_ref):
    tb = h_ref.shape[1]
    c = pl.program_id(2)

    @pl.when(c == 0)
    def _():
        xn = _modulated_norm(h_ref[0], nw_ref[...], sc_ref[0], sh_ref[0])
        xb_ref[...] = xn.astype(BF16)
        acc_ref[...] = jnp.zeros_like(acc_ref)
        gate = gt_ref[...]
        ghi = gate.astype(BF16).astype(F32)
        ghi_ref[...] = ghi
        glo_ref[...] = gate - ghi
        sub = lax.broadcasted_iota(I32, (N_KEYS, LANES), 0)

        def build(t, carry):
            eq1 = sub == e1_ref[pl.ds(t, 1), :]
            eq2 = sub == e2_ref[pl.ds(t, 1), :]
            ghi_t = ghi_ref[pl.ds(t, 1), :]
            glo_t = glo_ref[pl.ds(t, 1), :]
            a = jnp.concatenate([jnp.where(eq1, ghi_t, 0.0), jnp.where(eq1, glo_t, 0.0)],
                                axis=1).astype(BF16)
            one = jnp.where(eq2, 1.0, 0.0)
            bm = jnp.concatenate([one, one], axis=1).astype(BF16)
            g_ref[pl.ds(pl.multiple_of(t * G_PITCH, SUBLANES), N_KEYS), :] = _dot_nt(a, bm)
            return carry

        lax.fori_loop(0, tb, build, 0, unroll=G_BUILD_UNROLL)

    xb = xb_ref[...]
    per_key = MIX_SUB // N_KEYS
    nsub = MIX_EXPERTS // MIX_SUB

    def hidden(j):
        return jnp.dot(xb, ut_ref[:, j * MIX_SUB:(j + 1) * MIX_SUB], preferred_element_type=F32)

    hid = hidden(0)
    acc = acc_ref[...]
    for j in range(nsub):
        nxt = hidden(j + 1) if j + 1 < nsub else None
        key0 = c * (MIX_EXPERTS // N_KEYS) + j * per_key
        gates = jnp.concatenate(
            [g_ref[pl.ds(key0 + i, tb, stride=G_PITCH), :] for i in range(per_key)], axis=1)
        p = (_gelu_exact(hid) * gates).astype(BF16)
        acc = acc + jnp.dot(p, v_ref[j * MIX_SUB:(j + 1) * MIX_SUB, :],
                            preferred_element_type=F32)
        hid = nxt
    acc_ref[...] = acc

    @pl.when(c == pl.num_programs(2) - 1)
    def _():
        h2 = h_ref[0] + g2_ref[0] * acc_ref[...]
        r = h2 * lax.rsqrt(jnp.mean(h2 * h2, axis=-1, keepdims=True) + EPS)
        o_ref[0] = r * fw_ref[...]


def _peer_mix(h, nw, sc, sh, g2, fw, e1, e2, gate, ut, v):
    b, s, d = h.shape
    tb = min(MIX_TOKENS, s)
    nb = s // tb
    ne = v.shape[0] // MIX_EXPERTS
    lanes = PEER_TOPK * PEER_HEADS
    tok = lambda bi, m, c: (bi * nb + m, 0)
    vec = lambda bi, m, c: (bi, 0, 0)
    return pl.pallas_call(
        _mix_kernel,
        grid=(b, nb, ne),
        in_specs=[pl.BlockSpec((1, tb, d), lambda bi, m, c: (bi, m, 0)),
                  pl.BlockSpec((1, d), lambda bi, m, c: (0, 0)),
                  pl.BlockSpec((1, 1, d), vec),
                  pl.BlockSpec((1, 1, d), vec),
                  pl.BlockSpec((1, 1, d), vec),
                  pl.BlockSpec((1, d), lambda bi, m, c: (0, 0)),
                  pl.BlockSpec((tb, lanes), tok),
                  pl.BlockSpec((tb, lanes), tok),
                  pl.BlockSpec((tb, lanes), tok),
                  pl.BlockSpec((d, MIX_EXPERTS), lambda bi, m, c: (0, c)),
                  pl.BlockSpec((MIX_EXPERTS, d), lambda bi, m, c: (c, 0))],
        out_specs=pl.BlockSpec((1, tb, d), lambda bi, m, c: (bi, m, 0)),
        out_shape=jax.ShapeDtypeStruct((b, s, d), F32),
        scratch_shapes=[pltpu.VMEM((tb, d), BF16),
                        pltpu.VMEM((tb, lanes), F32),
                        pltpu.VMEM((tb, lanes), F32),
                        pltpu.VMEM((tb * G_PITCH, LANES), F32),
                        pltpu.VMEM((tb, d), F32)],
        compiler_params=_cparams(("parallel", "parallel", "arbitrary")),
        name="peer_mix",
    )(h, nw, sc, sh, g2, fw, e1, e2, gate, ut, v)


def _pad_lanes(vec, width=LANES):
    return jnp.zeros((1, width), F32).at[0, :vec.shape[0]].set(vec)


def kernel(x, c, positions, ada_w, ada_b, norm1_w, norm2_w, final_norm_w, w_in, conv_w, conv_b,
           dt_bias, a_log, d_skip, ssd_norm_w, w_ssd_o, lambda_q1, lambda_k1, lambda_q2,
           lambda_k2, attn_norm_w, w_attn_o, rel_bias, w_out, peer_wq, peer_key1, peer_key2,
           peer_u, peer_v):
    b, s, d = x.shape
    l = 0

    c8 = jnp.zeros((SUBLANES, d), F32).at[:b].set(c)
    ada = _ada(c8, ada_w[l], ada_b[l][None, :])[:b]
    sh1, sc1, g1, sh2, sc2, g2 = [m[:, None, :] for m in jnp.split(ada, 6, axis=-1)]

    wl = w_in[l]
    w_f = jnp.concatenate([wl[:, OFF_XBC:OFF_DT], wl[:, OFF_Z:OFF_XBC], wl[:, OFF_GS:IN_COLS]],
                          axis=1).astype(BF16)
    w_k = wl[:, OFF_K:OFF_V].astype(BF16)
    w_qv_t = jnp.concatenate([wl[:, OFF_Q:OFF_K], wl[:, OFF_V:OFF_GS]], axis=1).T.astype(BF16)
    w_dt = jnp.zeros((d, LANES), F32).at[:, :SSD_HEADS].set(wl[:, OFF_DT:OFF_Q])
    nw1 = norm1_w[l][None, :]

    pf = _norm_proj(x, nw1, sc1, sh1, w_f, F32, 512, 1024, "proj_f32")
    kk = _norm_proj(x, nw1, sc1, sh1, w_k, BF16, 512, 1024, "proj_k")
    nqk = ATTN_HEADS * 2 * ATTN_HEAD_DIM
    row_scale = jnp.concatenate([jnp.full((nqk, 1), ATTN_HEAD_DIM ** -0.5 * LOG2E, F32),
                                 jnp.ones((ATTN_HEADS * ATTN_V_DIM, 1), F32)])
    qvt = _norm_proj_t(x, nw1, sc1, sh1, w_qv_t, row_scale, BF16, 512, 1024, "proj_qvt")
    dtraw = _norm_proj(x, nw1, sc1, sh1, w_dt, F32, 512, LANES, "proj_dt", precision=HIGHEST)

    dsk = jnp.repeat(d_skip[l], SSD_HEAD_DIM)[None, :]
    yn = _ssd(pf, dtraw, conv_w[l], conv_b[l][None, :], _pad_lanes(dt_bias[l]),
              _pad_lanes(a_log[l]), dsk, ssd_norm_w[l][None, :])

    lamv = jnp.zeros((SUBLANES, LANES), F32)
    for i, v_ in enumerate((lambda_q1, lambda_k1, lambda_q2, lambda_k2)):
        lamv = lamv.at[i, :ATTN_HEAD_DIM].set(v_[l])
    an = _attn(qvt, kk, positions, rel_bias, lamv, attn_norm_w[l][:, None])

    h1 = _merge(x, yn, an, pf, g1, w_ssd_o[l].astype(BF16), w_attn_o[l].astype(BF16),
                w_out[l].astype(BF16))

    nw2 = norm2_w[l][None, :]
    q = _norm_proj(h1, nw2, sc2, sh2, peer_wq[l].astype(BF16), F32, 512, 1024, "peer_q")
    keys = jnp.stack([peer_key1[l], peer_key2[l]])
    kperm = jnp.einsum('ahkd,ab,hg->kahgbd', keys, jnp.eye(2, dtype=F32),
                       jnp.eye(PEER_HEADS, dtype=F32))
    kperm = kperm.reshape(N_KEYS * N_INST, PEER_HEADS * PEER_DK).astype(BF16)
    e1, e2, gate = _peer_topk(q.reshape(b * s, PEER_HEADS * PEER_DK), kperm)
    return _peer_mix(h1, nw2, sc2, sh2, g2, final_norm_w[None, :], e1, e2, gate,
                     peer_u[l].T.astype(BF16), peer_v[l].astype(BF16))
```

```python
import functools
import math

import jax
import jax.numpy as jnp
from jax import lax
from jax.experimental import pallas as pl
from jax.experimental.pallas import tpu as pltpu

F32 = jnp.float32
BF16 = jnp.bfloat16
I32 = jnp.int32
U32 = jnp.uint32
HIGHEST = lax.Precision.HIGHEST

D_MODEL = 1024
D_INNER = 2048
SSD_HEADS = 32
SSD_HEAD_DIM = 64
SSD_GROUPS = 8
SSD_STATE = 128
CONV_K = 4
CONV_CH = D_INNER + 2 * SSD_GROUPS * SSD_STATE
CHUNK = 128
ATTN_HEADS = 8
ATTN_HEAD_DIM = 64
ATTN_V_DIM = 2 * ATTN_HEAD_DIM
NUM_BUCKETS = 32
MAX_DISTANCE = 128
PEER_HEADS = 8
N_KEYS = 128
N_EXPERTS = N_KEYS * N_KEYS
PEER_DK = 128
PEER_TOPK = 16
EPS = 1e-6
LAMBDA_INIT = 0.8 - 0.6 * math.exp(0.0)

LANES = 128
SUBLANES = 8
VMEM_LIMIT = 48 * 1024 * 1024

OFF_Z = 0
OFF_XBC = OFF_Z + D_INNER
OFF_DT = OFF_XBC + CONV_CH
OFF_Q = OFF_DT + SSD_HEADS
OFF_K = OFF_Q + ATTN_HEADS * 2 * ATTN_HEAD_DIM
OFF_V = OFF_K + ATTN_HEADS * 2 * ATTN_HEAD_DIM
OFF_GS = OFF_V + ATTN_HEADS * ATTN_V_DIM
OFF_GA = OFF_GS + D_MODEL
IN_COLS = OFF_GA + D_MODEL


def _t5_far_distance():
    max_exact = NUM_BUCKETS // 2
    n = max_exact
    while True:
        b = max_exact + int(math.log(n / max_exact) / math.log(MAX_DISTANCE / max_exact)
                            * (NUM_BUCKETS - max_exact))
        if b >= NUM_BUCKETS - 1:
            return n
        n += 1


N_FAR = _t5_far_distance()
assert N_FAR <= LANES


def _sigmoid(x):
    return 1.0 / (1.0 + jnp.exp(-x))


def _silu(x):
    return x * _sigmoid(x)


def _cparams(sem):
    return pltpu.CompilerParams(dimension_semantics=sem, vmem_limit_bytes=VMEM_LIMIT)


def _dot_nt(a, b):
    return lax.dot_general(a, b, (((1,), (1,)), ((), ())), preferred_element_type=F32)


def _ada_kernel(c_ref, w_ref, b_ref, o_ref):
    o_ref[...] = jnp.dot(_silu(c_ref[...]), w_ref[...], preferred_element_type=F32,
                         precision=HIGHEST) + b_ref[...]


def _ada(c8, w, b):
    n = w.shape[1]
    tn = D_MODEL
    return pl.pallas_call(
        _ada_kernel,
        grid=(n // tn,),
        in_specs=[pl.BlockSpec((SUBLANES, D_MODEL), lambda j: (0, 0)),
                  pl.BlockSpec((D_MODEL, tn), lambda j: (0, j)),
                  pl.BlockSpec((1, tn), lambda j: (0, j))],
        out_specs=pl.BlockSpec((SUBLANES, tn), lambda j: (0, j)),
        out_shape=jax.ShapeDtypeStruct((SUBLANES, n), F32),
        compiler_params=_cparams(("arbitrary",)),
        name="ada",
    )(c8, w, b)


def _modulated_norm(x, nw, sc, sh):
    r = x * lax.rsqrt(jnp.mean(x * x, axis=-1, keepdims=True) + EPS) * nw
    return r * (1.0 + sc) + sh


def _norm_proj_kernel(x_ref, nw_ref, sc_ref, sh_ref, w_ref, o_ref, xn_ref, *, precision):
    @pl.when(pl.program_id(2) == 0)
    def _():
        xn = _modulated_norm(x_ref[0], nw_ref[...], sc_ref[0], sh_ref[0])
        xn_ref[...] = xn.astype(xn_ref.dtype)

    o_ref[0] = jnp.dot(xn_ref[...], w_ref[...], preferred_element_type=F32,
                       precision=precision).astype(o_ref.dtype)


def _norm_proj(x, nw, sc, sh, w, out_dtype, tm, tn, name, precision=None):
    b, s, d = x.shape
    n = w.shape[1]
    tm = min(tm, s)
    tn = min(tn, n)
    return pl.pallas_call(
        functools.partial(_norm_proj_kernel, precision=precision),
        grid=(b, s // tm, n // tn),
        in_specs=[pl.BlockSpec((1, tm, d), lambda bi, m, j: (bi, m, 0)),
                  pl.BlockSpec((1, d), lambda bi, m, j: (0, 0)),
                  pl.BlockSpec((1, 1, d), lambda bi, m, j: (bi, 0, 0)),
                  pl.BlockSpec((1, 1, d), lambda bi, m, j: (bi, 0, 0)),
                  pl.BlockSpec((d, tn), lambda bi, m, j: (0, j))],
        out_specs=pl.BlockSpec((1, tm, tn), lambda bi, m, j: (bi, m, j)),
        out_shape=jax.ShapeDtypeStruct((b, s, n), out_dtype),
        scratch_shapes=[pltpu.VMEM((tm, d), w.dtype)],
        compiler_params=_cparams(("parallel", "parallel", "arbitrary")),
        name=name,
    )(x, nw, sc, sh, w)


def _norm_proj_t_kernel(x_ref, nw_ref, sc_ref, sh_ref, wt_ref, rs_ref, o_ref, xn_ref):
    @pl.when(pl.program_id(2) == 0)
    def _():
        xn = _modulated_norm(x_ref[0], nw_ref[...], sc_ref[0], sh_ref[0])
        xn_ref[...] = xn.astype(xn_ref.dtype)

    o_ref[0] = (_dot_nt(wt_ref[...], xn_ref[...]) * rs_ref[...]).astype(o_ref.dtype)


def _norm_proj_t(x, nw, sc, sh, wt, row_scale, out_dtype, tm, tn, name):
    b, s, d = x.shape
    n = wt.shape[0]
    tm = min(tm, s)
    tn = min(tn, n)
    return pl.pallas_call(
        _norm_proj_t_kernel,
        grid=(b, s // tm, n // tn),
        in_specs=[pl.BlockSpec((1, tm, d), lambda bi, m, j: (bi, m, 0)),
                  pl.BlockSpec((1, d), lambda bi, m, j: (0, 0)),
                  pl.BlockSpec((1, 1, d), lambda bi, m, j: (bi, 0, 0)),
                  pl.BlockSpec((1, 1, d), lambda bi, m, j: (bi, 0, 0)),
                  pl.BlockSpec((tn, d), lambda bi, m, j: (j, 0)),
                  pl.BlockSpec((tn, 1), lambda bi, m, j: (j, 0))],
        out_specs=pl.BlockSpec((1, tn, tm), lambda bi, m, j: (bi, j, m)),
        out_shape=jax.ShapeDtypeStruct((b, n, s), out_dtype),
        scratch_shapes=[pltpu.VMEM((tm, d), wt.dtype)],
        compiler_params=_cparams(("parallel", "parallel", "arbitrary")),
        name=name,
    )(x, nw, sc, sh, wt, row_scale)


CONV_HALO = SUBLANES
CONV_COLS = 512
GROUP_COLS = D_INNER // SSD_GROUPS
HEADS_PER_GROUP = SSD_HEADS // SSD_GROUPS
PF_COLS = CONV_CH + D_INNER + 2 * D_MODEL
PF_Z_BLOCK = CONV_CH // D_INNER
PF_GS_BLOCK = (CONV_CH + D_INNER) // D_MODEL
PF_GA_BLOCK = PF_GS_BLOCK + 1


def _softplus(x):
    return jnp.maximum(x, 0.0) + jnp.log(1.0 + jnp.exp(-jnp.abs(x)))


def _ssd_kernel(xbc_ref, z_ref, dt_ref, cw_ref, cb_ref, dtb_ref, alog_ref, dsk_ref, nw_ref,
                o_ref, state_ref, win_ref, act_ref):
    L = CHUNK

    @pl.when(pl.program_id(1) == 0)
    def _():
        state_ref[...] = jnp.zeros_like(state_ref)
        win_ref[0:CONV_HALO, :] = jnp.zeros((CONV_HALO, CONV_CH), F32)

    win_ref[CONV_HALO:CONV_HALO + L, :] = xbc_ref[0]
    base = CONV_HALO - (CONV_K - 1)
    for j in range(CONV_CH // CONV_COLS):
        cs = slice(j * CONV_COLS, (j + 1) * CONV_COLS)
        acc = cb_ref[:, cs] + cw_ref[0:1, cs] * win_ref[base:base + L, cs]
        for k in range(1, CONV_K):
            acc = acc + cw_ref[k:k + 1, cs] * win_ref[base + k:base + k + L, cs]
        act_ref[:, cs] = _silu(acc)
    win_ref[0:CONV_HALO, :] = win_ref[L:L + CONV_HALO, :]

    dtv = _softplus(dt_ref[0] + dtb_ref[...])
    a = -jnp.exp(alog_ref[...])
    row = lax.broadcasted_iota(I32, (L, L), 0)
    col = lax.broadcasted_iota(I32, (L, L), 1)
    tril = row >= col
    a_cum = jnp.dot(tril.astype(F32), dtv * a, preferred_element_type=F32,
                    precision=HIGHEST)
    a_cum_t = a_cum.T
    dt_t = dtv.T
    a_last = a_cum[L - 1:L, :]
    to_end = jnp.exp(a_last - a_cum) * dtv
    ea = jnp.exp(a_cum)
    cdec = jnp.exp(a_last)

    head_of_col = lax.broadcasted_iota(I32, (L, GROUP_COLS), 1) // SSD_HEAD_DIM
    neg_inf = jnp.float32(-jnp.inf)

    for g in range(SSD_GROUPS):
        xs = slice(g * GROUP_COLS, (g + 1) * GROUP_COLS)
        bs = slice(D_INNER + g * SSD_STATE, D_INNER + (g + 1) * SSD_STATE)
        cs = slice(D_INNER + (SSD_GROUPS + g) * SSD_STATE,
                   D_INNER + (SSD_GROUPS + g + 1) * SSD_STATE)
        xg = act_ref[:, xs]
        bg = act_ref[:, bs]
        xb = xg.astype(BF16)
        bb = bg.astype(BF16)
        cb16 = act_ref[:, cs].astype(BF16)
        cbm = _dot_nt(cb16, bb)
        prev_t = state_ref[:, xs]
        y_off = jnp.dot(cb16, prev_t.astype(BF16), preferred_element_type=F32)

        y = jnp.zeros((L, GROUP_COLS), F32)
        ea_g = jnp.zeros((L, GROUP_COLS), F32)
        te_g = jnp.zeros((L, GROUP_COLS), F32)
        cd_g = jnp.zeros((1, GROUP_COLS), F32)
        for r in range(HEADS_PER_GROUP):
            h = g * HEADS_PER_GROUP + r
            seg = a_cum[:, h:h + 1] - a_cum_t[h:h + 1, :]
            decay = jnp.exp(jnp.where(tril, seg, neg_inf))
            wts = (cbm * decay * dt_t[h:h + 1, :]).astype(BF16)
            yd = jnp.dot(wts, xb, preferred_element_type=F32)
            sel = head_of_col == r
            y = jnp.where(sel, yd, y)
            ea_g = jnp.where(sel, ea[:, h:h + 1], ea_g)
            te_g = jnp.where(sel, to_end[:, h:h + 1], te_g)
            cd_g = jnp.where(sel[0:1, :], cdec[:, h:h + 1], cd_g)
        y = y + y_off * ea_g + dsk_ref[:, xs] * xg
        xs_scaled = (xg * te_g).astype(BF16)
        st = jnp.dot(bg.T.astype(BF16), xs_scaled, preferred_element_type=F32)
        state_ref[:, xs] = prev_t * cd_g + st

        yz = y * _silu(z_ref[0, :, xs])
        ms = jnp.mean(yz * yz, axis=-1, keepdims=True)
        o_ref[0, :, xs] = (yz * lax.rsqrt(ms + EPS) * nw_ref[:, xs]).astype(o_ref.dtype)


def _ssd(pf, dtraw, cw, cb, dtb, alog, dsk, nw):
    b, s, _ = pf.shape
    nc = s // CHUNK
    const2 = lambda bi, c: (0, 0)
    return pl.pallas_call(
        _ssd_kernel,
        grid=(b, nc),
        in_specs=[pl.BlockSpec((1, CHUNK, CONV_CH), lambda bi, c: (bi, c, 0)),
                  pl.BlockSpec((1, CHUNK, D_INNER), lambda bi, c: (bi, c, PF_Z_BLOCK)),
                  pl.BlockSpec((1, CHUNK, LANES), lambda bi, c: (bi, c, 0)),
                  pl.BlockSpec((CONV_K, CONV_CH), const2),
                  pl.BlockSpec((1, CONV_CH), const2),
                  pl.BlockSpec((1, LANES), const2),
                  pl.BlockSpec((1, LANES), const2),
                  pl.BlockSpec((1, D_INNER), const2),
                  pl.BlockSpec((1, D_INNER), const2)],
        out_specs=pl.BlockSpec((1, CHUNK, D_INNER), lambda bi, c: (bi, c, 0)),
        out_shape=jax.ShapeDtypeStruct((b, s, D_INNER), BF16),
        scratch_shapes=[pltpu.VMEM((SSD_STATE, D_INNER), F32),
                        pltpu.VMEM((CONV_HALO + CHUNK, CONV_CH), F32),
                        pltpu.VMEM((CHUNK, CONV_CH), F32)],
        compiler_params=_cparams(("parallel", "arbitrary")),
        name="ssd",
    )(pf, pf, dtraw, cw, cb, dtb, alog, dsk, nw)


ATT_TILE = 512
LOG2E = math.log2(math.e)
BF16_ROWS = 2 * SUBLANES
ACC_ROWS = ATTN_V_DIM + BF16_ROWS


def _t5_bias_lut(rb_row):
    n = lax.broadcasted_iota(I32, (LANES, LANES), 1)
    bkt = lax.broadcasted_iota(I32, (LANES, LANES), 0)
    max_exact = NUM_BUCKETS // 2
    nf = jnp.maximum(n, 1).astype(F32)
    large = max_exact + (jnp.log(nf / max_exact) / math.log(MAX_DISTANCE / max_exact)
                         * (NUM_BUCKETS - max_exact)).astype(I32)
    large = jnp.minimum(large, NUM_BUCKETS - 1)
    bucket = jnp.where(n < max_exact, n, large)
    onehot = (bucket == bkt).astype(F32)
    lut = jnp.dot(jnp.broadcast_to(rb_row, (SUBLANES, LANES)), onehot,
                  preferred_element_type=F32, precision=HIGHEST)
    return lut[0:1, :]


def _attn_kernel(pmin_ref, pmax_ref, qt_ref, k_ref, vt_ref, pos_ref, rb_ref,
                 lamv_ref, nw_ref, o_ref, qm_ref, m_ref, acc_ref, sa_ref, sb_ref, *, tile):
    T = tile
    bi = pl.program_id(0)
    qi = pl.program_id(2)
    neg_inf = jnp.float32(-jnp.inf)
    nck = T // LANES

    qt = qt_ref[0]
    feat = lax.broadcasted_iota(I32, (2 * ATTN_HEAD_DIM, T), 0)
    zero = jnp.zeros_like(qt)
    qm_ref[0] = jnp.where(feat < ATTN_HEAD_DIM, qt, zero)
    qm_ref[1] = jnp.where(feat >= ATTN_HEAD_DIM, qt, zero)
    m_ref[...] = jnp.full_like(m_ref, neg_inf)
    acc_ref[...] = jnp.zeros_like(acc_ref)
    ones_rows = (lax.broadcasted_iota(I32, (BF16_ROWS, T), 0) == 0).astype(BF16)

    lut = _t5_bias_lut(rb_ref[0]) * LOG2E
    c_far = lut[:, LANES - 1:LANES]
    lut_b = jnp.broadcast_to(lut, (LANES, LANES))
    qoff = pl.multiple_of(qi * T, T)
    pos_q = pos_ref[0, :, pl.ds(qoff, T)]

    def scores(ki, dst_ref):
        koff = pl.multiple_of(ki * T, T)
        k = k_ref[0, pl.ds(koff, T), :]
        for i in range(2):
            dst_ref[i] = jnp.dot(k, qm_ref[i], preferred_element_type=F32)

    def consume(ki, src_ref, near, diag):
        koff = pl.multiple_of(ki * T, T)
        vt = jnp.concatenate([vt_ref[0, :, pl.ds(koff, T)], ones_rows], axis=0)
        if near:
            rows = []
            for kc in range(nck):
                pk_row = pos_ref[0, :, pl.ds(koff + kc * LANES, LANES)]
                pk = jnp.broadcast_to(pk_row, (LANES, LANES)).T
                blocks = []
                for qc in range(nck):
                    dist = jnp.clip(pos_q[:, qc * LANES:(qc + 1) * LANES] - pk, 0, LANES - 1)
                    blocks.append(jnp.take_along_axis(lut_b, dist, axis=1))
                rows.append(jnp.concatenate(blocks, axis=1))
            bias = jnp.concatenate(rows, axis=0)
            if diag:
                kr = lax.broadcasted_iota(I32, (T, T), 0)
                qc_ = lax.broadcasted_iota(I32, (T, T), 1)
                bias = jnp.where(kr <= qc_, bias, neg_inf)
        for i in range(2):
            s = src_ref[i]
            if near:
                s = s + bias
                m_tile = jnp.max(s, axis=0, keepdims=True)
            else:
                m_tile = jnp.max(s, axis=0, keepdims=True) + c_far
            m_prev = m_ref[i]
            m_new = jnp.maximum(m_prev, m_tile)
            shift = m_new if near else m_new - c_far
            p = jnp.exp2(s - shift)
            alpha = jnp.exp2(m_prev - m_new)
            acc_ref[i] = alpha * acc_ref[i] + jnp.dot(vt, p.astype(vt.dtype),
                                                      preferred_element_type=F32)
            m_ref[i] = m_new

    def step(t, src_ref, dst_ref):
        near = pmin_ref[bi, qi] - pmax_ref[bi, t] < N_FAR

        @pl.when(near)
        def _():
            scores(t + 1, dst_ref)
            consume(t, src_ref, True, False)

        @pl.when(jnp.logical_not(near))
        def _():
            scores(t + 1, dst_ref)
            consume(t, src_ref, False, False)

    def body(jj, carry):
        step(2 * jj, sa_ref, sb_ref)

        @pl.when(2 * jj + 1 < qi)
        def _():
            step(2 * jj + 1, sb_ref, sa_ref)

        return carry

    scores(0, sa_ref)
    lax.fori_loop(0, (qi + 1) // 2, body, 0)

    @pl.when(qi % 2 == 0)
    def _():
        consume(qi, sa_ref, True, True)

    @pl.when(qi % 2 == 1)
    def _():
        consume(qi, sb_ref, True, True)

    lv = lamv_ref[...]
    lam = (jnp.exp(jnp.sum(lv[0:1, :] * lv[1:2, :], axis=-1, keepdims=True))
           - jnp.exp(jnp.sum(lv[2:3, :] * lv[3:4, :], axis=-1, keepdims=True))
           + LAMBDA_INIT)
    nv = ATTN_V_DIM
    att = (acc_ref[0, 0:nv, :] / acc_ref[0, nv:nv + 1, :]
           - lam * (acc_ref[1, 0:nv, :] / acc_ref[1, nv:nv + 1, :]))
    ms = jnp.mean(att * att, axis=0, keepdims=True)
    y = att * lax.rsqrt(ms + EPS) * nw_ref[...] * (1.0 - LAMBDA_INIT)
    o_ref[0] = y.T.astype(o_ref.dtype)


def _attn(qvt, kk, positions, rel_bias, lamv, nw_col):
    b, s, _ = kk.shape
    T = min(ATT_TILE, s)
    nq = s // T
    H = ATTN_HEADS
    pos_blk = positions.reshape(b, nq, T)
    pmin = jnp.min(pos_blk, axis=-1)
    pmax = jnp.max(pos_blk, axis=-1)
    posr = positions.reshape(b, 1, s)
    rb = jnp.zeros((H, 1, LANES), F32).at[:, 0, :NUM_BUCKETS].set(rel_bias.T)
    grid_spec = pltpu.PrefetchScalarGridSpec(
        num_scalar_prefetch=2,
        grid=(b, H, nq),
        in_specs=[pl.BlockSpec((1, LANES, T), lambda bi, h, qi, *_: (bi, h, qi)),
                  pl.BlockSpec((1, s, LANES), lambda bi, h, qi, *_: (bi, 0, h)),
                  pl.BlockSpec((1, LANES, s), lambda bi, h, qi, *_: (bi, H + h, 0)),
                  pl.BlockSpec((1, 1, s), lambda bi, h, qi, *_: (bi, 0, 0)),
                  pl.BlockSpec((1, 1, LANES), lambda bi, h, qi, *_: (h, 0, 0)),
                  pl.BlockSpec((SUBLANES, LANES), lambda bi, h, qi, *_: (0, 0)),
                  pl.BlockSpec((LANES, 1), lambda bi, h, qi, *_: (0, 0))],
        out_specs=pl.BlockSpec((1, T, LANES), lambda bi, h, qi, *_: (bi, qi, h)),
        scratch_shapes=[pltpu.VMEM((2, LANES, T), BF16),
                        pltpu.VMEM((2, 1, T), F32),
                        pltpu.VMEM((2, ACC_ROWS, T), F32),
                        pltpu.VMEM((2, T, T), F32),
                        pltpu.VMEM((2, T, T), F32)])
    return pl.pallas_call(
        functools.partial(_attn_kernel, tile=T),
        grid_spec=grid_spec,
        out_shape=jax.ShapeDtypeStruct((b, s, H * ATTN_V_DIM), BF16),
        compiler_params=_cparams(("parallel", "parallel", "arbitrary")),
        name="attn",
    )(pmin, pmax, qvt, kk, qvt, posr, rb, lamv, nw_col)


def _merge_kernel(x_ref, ys_ref, ya_ref, gs_ref, ga_ref, g1_ref, ws_ref, wa_ref, wo_ref, o_ref):
    y_ssd = jnp.dot(ys_ref[0], ws_ref[...], preferred_element_type=F32)
    y_att = jnp.dot(ya_ref[0], wa_ref[...], preferred_element_type=F32)
    mix_in = _sigmoid(gs_ref[0]) * y_ssd + _sigmoid(ga_ref[0]) * y_att
    mix = jnp.dot(mix_in.astype(BF16), wo_ref[...], preferred_element_type=F32)
    o_ref[0] = x_ref[0] + g1_ref[0] * mix


def _merge(x, ys, ya, pf, g1, ws, wa, wo, tm=512):
    b, s, d = x.shape
    tm = min(tm, s)
    const2 = lambda bi, m: (0, 0)
    return pl.pallas_call(
        _merge_kernel,
        grid=(b, s // tm),
        in_specs=[pl.BlockSpec((1, tm, d), lambda bi, m: (bi, m, 0)),
                  pl.BlockSpec((1, tm, D_INNER), lambda bi, m: (bi, m, 0)),
                  pl.BlockSpec((1, tm, d), lambda bi, m: (bi, m, 0)),
                  pl.BlockSpec((1, tm, d), lambda bi, m: (bi, m, PF_GS_BLOCK)),
                  pl.BlockSpec((1, tm, d), lambda bi, m: (bi, m, PF_GA_BLOCK)),
                  pl.BlockSpec((1, 1, d), lambda bi, m: (bi, 0, 0)),
                  pl.BlockSpec((D_INNER, d), const2),
                  pl.BlockSpec((d, d), const2),
                  pl.BlockSpec((d, d), const2)],
        out_specs=pl.BlockSpec((1, tm, d), lambda bi, m: (bi, m, 0)),
        out_shape=jax.ShapeDtypeStruct((b, s, d), F32),
        compiler_params=_cparams(("parallel", "parallel")),
        name="merge",
    )(x, ys, ya, pf, pf, g1, ws, wa, wo)


TOPK_TOKENS = 256
N_INST = 2 * PEER_HEADS
PAIR_CANDS = [(a, c) for a in range(PEER_TOPK) for c in range(PEER_TOPK)
              if (a + 1) * (c + 1) <= PEER_TOPK]


def _topk_kernel(q_ref, kp_ref, e1_ref, e2_ref, g_ref, s_ref, v_ref, i_ref, cs_ref, ce_ref,
                 os_ref, oe_ref):
    tn = q_ref.shape[0]
    neg_inf = jnp.float32(-jnp.inf)
    s_ref[...] = _dot_nt(kp_ref[...], q_ref[...].astype(BF16))

    def level1(it, prev):
        best = jnp.full((N_INST, tn), neg_inf, F32)
        bidx = jnp.zeros((N_INST, tn), I32)
        for key in range(N_KEYS):
            rows = slice(key * N_INST, (key + 1) * N_INST)
            sk = jnp.where(prev == key, neg_inf, s_ref[rows, :])
            s_ref[rows, :] = sk
            upd = sk > best
            best = jnp.where(upd, sk, best)
            bidx = jnp.where(upd, key, bidx)
        v_ref[it] = best
        i_ref[it] = bidx
        return bidx

    lax.fori_loop(0, PEER_TOPK, level1, jnp.full((N_INST, tn), -1, I32))

    H = PEER_HEADS
    for ci, (a, c) in enumerate(PAIR_CANDS):
        cs_ref[ci] = v_ref[a, 0:H, :] + v_ref[c, H:2 * H, :]
        ce_ref[ci] = i_ref[a, 0:H, :] * N_KEYS + i_ref[c, H:2 * H, :]

    def level2(it, prev):
        best = jnp.full((H, tn), neg_inf, F32)
        bci = jnp.zeros((H, tn), I32)
        bex = jnp.zeros((H, tn), I32)
        for ci in range(len(PAIR_CANDS)):
            sk = jnp.where(prev == ci, neg_inf, cs_ref[ci])
            cs_ref[ci] = sk
            upd = sk > best
            best = jnp.where(upd, sk, best)
            bci = jnp.where(upd, ci, bci)
            bex = jnp.where(upd, ce_ref[ci], bex)
        os_ref[it] = best
        oe_ref[it] = bex
        return bci

    lax.fori_loop(0, PEER_TOPK, level2, jnp.full((H, tn), -1, I32))

    sc = os_ref[...]
    ex = jnp.exp(sc - sc[0:1])
    gate = ex / jnp.sum(ex, axis=0, keepdims=True)
    ex_i = oe_ref[...]
    rows = PEER_TOPK * H
    g_ref[...] = gate.reshape(rows, tn).T
    key_bits = N_KEYS.bit_length() - 1
    e1 = lax.shift_right_logical(ex_i, key_bits).astype(F32).reshape(rows, tn)
    e2 = lax.bitwise_and(ex_i, N_KEYS - 1).astype(F32).reshape(rows, tn)
    e1_ref[...] = e1.T.astype(I32)
    e2_ref[...] = e2.T.astype(I32)


def _peer_topk(q, kperm):
    t, d = q.shape
    tn = min(TOPK_TOKENS, t)
    lanes = PEER_TOPK * PEER_HEADS
    ncand = len(PAIR_CANDS)
    out = jax.ShapeDtypeStruct((t, lanes), I32)
    return pl.pallas_call(
        _topk_kernel,
        grid=(t // tn,),
        in_specs=[pl.BlockSpec((tn, d), lambda i: (i, 0)),
                  pl.BlockSpec(kperm.shape, lambda i: (0, 0))],
        out_specs=[pl.BlockSpec((tn, lanes), lambda i: (i, 0))] * 3,
        out_shape=[out, out, jax.ShapeDtypeStruct((t, lanes), F32)],
        scratch_shapes=[pltpu.VMEM((N_KEYS * N_INST, tn), F32),
                        pltpu.VMEM((PEER_TOPK, N_INST, tn), F32),
                        pltpu.VMEM((PEER_TOPK, N_INST, tn), I32),
                        pltpu.VMEM((ncand, PEER_HEADS, tn), F32),
                        pltpu.VMEM((ncand, PEER_HEADS, tn), I32),
                        pltpu.VMEM((PEER_TOPK, PEER_HEADS, tn), F32),
                        pltpu.VMEM((PEER_TOPK, PEER_HEADS, tn), I32)],
        compiler_params=_cparams(("parallel",)),
        name="peer_topk",
    )(q, kperm)


MIX_TOKENS = 512
MIX_EXPERTS = 1024
MIX_SUB = 2 * N_KEYS
G_HALF = N_KEYS // 2
G_PITCH = G_HALF + SUBLANES
G_BUILD_UNROLL = 16


def _expert_sweep_order(w):
    d = w.shape[-1]
    return w.reshape(2, G_HALF, N_KEYS, d).transpose(1, 0, 2, 3).reshape(N_EXPERTS, d)


def _gelu_exact(x):
    return 0.5 * x * (1.0 + lax.erf(x * (1.0 / math.sqrt(2.0))))


def _mix_kernel(h_ref, nw_ref, sc_ref, sh_ref, g2_ref, fw_ref, e1_ref, e2_ref, gt_ref,
                ut_ref, v_ref, o_ref, xb_ref, g_ref, acc_ref):
    tb = h_ref.shape[1]
    c = pl.program_id(2)
    hi_mask = jnp.uint32(0xFFFF0000)

    @pl.when(c == 0)
    def _():
        xn = _modulated_norm(h_ref[0], nw_ref[...], sc_ref[0], sh_ref[0])
        xb_ref[...] = xn.astype(BF16)
        acc_ref[...] = jnp.zeros_like(acc_ref)
        sub = lax.broadcasted_iota(I32, (N_KEYS, LANES), 0)

        def build(t, carry):
            eq1 = sub == e1_ref[pl.ds(t, 1), :]
            eq2 = sub == e2_ref[pl.ds(t, 1), :]
            a = jnp.where(eq1, gt_ref[pl.ds(t, 1), :], 0.0).astype(BF16)
            bm = jnp.where(eq2, 1.0, 0.0).astype(BF16)
            gm = _dot_nt(a, bm)
            top = lax.bitcast_convert_type(gm[0:G_HALF].astype(BF16).astype(F32), U32)
            bot = lax.bitcast_convert_type(gm[G_HALF:N_KEYS].astype(BF16).astype(F32), U32)
            g_ref[pl.ds(pl.multiple_of(t * G_PITCH, SUBLANES), G_HALF), :] = (
                top | lax.shift_right_logical(bot, jnp.uint32(16)))
            return carry

        lax.fori_loop(0, tb, build, 0, unroll=G_BUILD_UNROLL)

    xb = xb_ref[...]
    nsub = MIX_EXPERTS // MIX_SUB

    def hidden(j):
        return jnp.dot(xb, ut_ref[:, j * MIX_SUB:(j + 1) * MIX_SUB], preferred_element_type=F32)

    hid = hidden(0)
    acc = acc_ref[...]
    for j in range(nsub):
        nxt = hidden(j + 1) if j + 1 < nsub else None
        words = g_ref[pl.ds(c * nsub + j, tb, stride=G_PITCH), :]
        gates = jnp.concatenate(
            [lax.bitcast_convert_type(words & hi_mask, F32),
             lax.bitcast_convert_type(lax.shift_left(words, jnp.uint32(16)), F32)], axis=1)
        p = (_gelu_exact(hid) * gates).astype(BF16)
        acc = acc + jnp.dot(p, v_ref[j * MIX_SUB:(j + 1) * MIX_SUB, :],
                            preferred_element_type=F32)
        hid = nxt
    acc_ref[...] = acc

    @pl.when(c == pl.num_programs(2) - 1)
    def _():
        h2 = h_ref[0] + g2_ref[0] * acc_ref[...]
        r = h2 * lax.rsqrt(jnp.mean(h2 * h2, axis=-1, keepdims=True) + EPS)
        o_ref[0] = r * fw_ref[...]


def _peer_mix(h, nw, sc, sh, g2, fw, e1, e2, gate, ut, v):
    b, s, d = h.shape
    tb = min(MIX_TOKENS, s)
    nb = s // tb
    ne = v.shape[0] // MIX_EXPERTS
    lanes = PEER_TOPK * PEER_HEADS
    tok = lambda bi, m, c: (bi * nb + m, 0)
    vec = lambda bi, m, c: (bi, 0, 0)
    return pl.pallas_call(
        _mix_kernel,
        grid=(b, nb, ne),
        in_specs=[pl.BlockSpec((1, tb, d), lambda bi, m, c: (bi, m, 0)),
                  pl.BlockSpec((1, d), lambda bi, m, c: (0, 0)),
                  pl.BlockSpec((1, 1, d), vec),
                  pl.BlockSpec((1, 1, d), vec),
                  pl.BlockSpec((1, 1, d), vec),
                  pl.BlockSpec((1, d), lambda bi, m, c: (0, 0)),
                  pl.BlockSpec((tb, lanes), tok),
                  pl.BlockSpec((tb, lanes), tok),
                  pl.BlockSpec((tb, lanes), tok),
                  pl.BlockSpec((d, MIX_EXPERTS), lambda bi, m, c: (0, c)),
                  pl.BlockSpec((MIX_EXPERTS, d), lambda bi, m, c: (c, 0))],
        out_specs=pl.BlockSpec((1, tb, d), lambda bi, m, c: (bi, m, 0)),
        out_shape=jax.ShapeDtypeStruct((b, s, d), F32),
        scratch_shapes=[pltpu.VMEM((tb, d), BF16),
                        pltpu.VMEM((tb * G_PITCH, LANES), U32),
                        pltpu.VMEM((tb, d), F32)],
        compiler_params=_cparams(("parallel", "parallel", "arbitrary")),
        name="peer_mix",
    )(h, nw, sc, sh, g2, fw, e1, e2, gate, ut, v)


def _pad_lanes(vec, width=LANES):
    return jnp.zeros((1, width), F32).at[0, :vec.shape[0]].set(vec)


def kernel(x, c, positions, ada_w, ada_b, norm1_w, norm2_w, final_norm_w, w_in, conv_w, conv_b,
           dt_bias, a_log, d_skip, ssd_norm_w, w_ssd_o, lambda_q1, lambda_k1, lambda_q2,
           lambda_k2, attn_norm_w, w_attn_o, rel_bias, w_out, peer_wq, peer_key1, peer_key2,
           peer_u, peer_v):
    b, s, d = x.shape
    l = 0

    c8 = jnp.zeros((SUBLANES, d), F32).at[:b].set(c)
    ada = _ada(c8, ada_w[l], ada_b[l][None, :])[:b]
    sh1, sc1, g1, sh2, sc2, g2 = [m[:, None, :] for m in jnp.split(ada, 6, axis=-1)]

    wl = w_in[l]
    w_f = jnp.concatenate([wl[:, OFF_XBC:OFF_DT], wl[:, OFF_Z:OFF_XBC], wl[:, OFF_GS:IN_COLS]],
                          axis=1).astype(BF16)
    w_k = wl[:, OFF_K:OFF_V].astype(BF16)
    w_qv_t = jnp.concatenate([wl[:, OFF_Q:OFF_K], wl[:, OFF_V:OFF_GS]], axis=1).T.astype(BF16)
    w_dt = jnp.zeros((d, LANES), F32).at[:, :SSD_HEADS].set(wl[:, OFF_DT:OFF_Q])
    nw1 = norm1_w[l][None, :]

    pf = _norm_proj(x, nw1, sc1, sh1, w_f, F32, 512, 1024, "proj_f32")
    kk = _norm_proj(x, nw1, sc1, sh1, w_k, BF16, 512, 1024, "proj_k")
    nqk = ATTN_HEADS * 2 * ATTN_HEAD_DIM
    row_scale = jnp.concatenate([jnp.full((nqk, 1), ATTN_HEAD_DIM ** -0.5 * LOG2E, F32),
                                 jnp.ones((ATTN_HEADS * ATTN_V_DIM, 1), F32)])
    qvt = _norm_proj_t(x, nw1, sc1, sh1, w_qv_t, row_scale, BF16, 512, 1024, "proj_qvt")
    dtraw = _norm_proj(x, nw1, sc1, sh1, w_dt, F32, 512, LANES, "proj_dt", precision=HIGHEST)

    dsk = jnp.repeat(d_skip[l], SSD_HEAD_DIM)[None, :]
    yn = _ssd(pf, dtraw, conv_w[l], conv_b[l][None, :], _pad_lanes(dt_bias[l]),
              _pad_lanes(a_log[l]), dsk, ssd_norm_w[l][None, :])

    lamv = jnp.zeros((SUBLANES, LANES), F32)
    for i, v_ in enumerate((lambda_q1, lambda_k1, lambda_q2, lambda_k2)):
        lamv = lamv.at[i, :ATTN_HEAD_DIM].set(v_[l])
    an = _attn(qvt, kk, positions, rel_bias, lamv, attn_norm_w[l][:, None])

    h1 = _merge(x, yn, an, pf, g1, w_ssd_o[l].astype(BF16), w_attn_o[l].astype(BF16),
                w_out[l].astype(BF16))

    nw2 = norm2_w[l][None, :]
    q = _norm_proj(h1, nw2, sc2, sh2, peer_wq[l].astype(BF16), F32, 512, 1024, "peer_q")
    keys = jnp.stack([peer_key1[l], peer_key2[l]])
    kperm = jnp.einsum('ahkd,ab,hg->kahgbd', keys, jnp.eye(2, dtype=F32),
                       jnp.eye(PEER_HEADS, dtype=F32))
    kperm = kperm.reshape(N_KEYS * N_INST, PEER_HEADS * PEER_DK).astype(BF16)
    e1, e2, gate = _peer_topk(q.reshape(b * s, PEER_HEADS * PEER_DK), kperm)
    return _peer_mix(h1, nw2, sc2, sh2, g2, final_norm_w[None, :], e1, e2, gate,
                     _expert_sweep_order(peer_u[l]).T.astype(BF16),
                     _expert_sweep_order(peer_v[l]).astype(BF16))
```

```python
import functools
import math

import jax
import jax.numpy as jnp
from jax import lax
from jax.experimental import pallas as pl
from jax.experimental.pallas import tpu as pltpu

F32 = jnp.float32
BF16 = jnp.bfloat16
I32 = jnp.int32
U32 = jnp.uint32
HIGHEST = lax.Precision.HIGHEST

D_MODEL = 1024
D_INNER = 2048
SSD_HEADS = 32
SSD_HEAD_DIM = 64
SSD_GROUPS = 8
SSD_STATE = 128
CONV_K = 4
CONV_CH = D_INNER + 2 * SSD_GROUPS * SSD_STATE
CHUNK = 128
ATTN_HEADS = 8
ATTN_HEAD_DIM = 64
ATTN_V_DIM = 2 * ATTN_HEAD_DIM
NUM_BUCKETS = 32
MAX_DISTANCE = 128
PEER_HEADS = 8
N_KEYS = 128
N_EXPERTS = N_KEYS * N_KEYS
PEER_DK = 128
PEER_TOPK = 16
EPS = 1e-6
LAMBDA_INIT = 0.8 - 0.6 * math.exp(0.0)

LANES = 128
SUBLANES = 8
VMEM_LIMIT = 48 * 1024 * 1024

OFF_Z = 0
OFF_XBC = OFF_Z + D_INNER
OFF_DT = OFF_XBC + CONV_CH
OFF_Q = OFF_DT + SSD_HEADS
OFF_K = OFF_Q + ATTN_HEADS * 2 * ATTN_HEAD_DIM
OFF_V = OFF_K + ATTN_HEADS * 2 * ATTN_HEAD_DIM
OFF_GS = OFF_V + ATTN_HEADS * ATTN_V_DIM
OFF_GA = OFF_GS + D_MODEL
IN_COLS = OFF_GA + D_MODEL


def _t5_far_distance():
    max_exact = NUM_BUCKETS // 2
    n = max_exact
    while True:
        b = max_exact + int(math.log(n / max_exact) / math.log(MAX_DISTANCE / max_exact)
                            * (NUM_BUCKETS - max_exact))
        if b >= NUM_BUCKETS - 1:
            return n
        n += 1


N_FAR = _t5_far_distance()
assert N_FAR <= LANES


def _sigmoid(x):
    return 1.0 / (1.0 + jnp.exp(-x))


def _silu(x):
    return x * _sigmoid(x)


def _cparams(sem):
    return pltpu.CompilerParams(dimension_semantics=sem, vmem_limit_bytes=VMEM_LIMIT)


def _dot_nt(a, b):
    return lax.dot_general(a, b, (((1,), (1,)), ((), ())), preferred_element_type=F32)


def _ada_kernel(c_ref, w_ref, b_ref, o_ref):
    o_ref[...] = jnp.dot(_silu(c_ref[...]), w_ref[...], preferred_element_type=F32,
                         precision=HIGHEST) + b_ref[...]


def _ada(c8, w, b):
    n = w.shape[1]
    tn = D_MODEL
    return pl.pallas_call(
        _ada_kernel,
        grid=(n // tn,),
        in_specs=[pl.BlockSpec((SUBLANES, D_MODEL), lambda j: (0, 0)),
                  pl.BlockSpec((D_MODEL, tn), lambda j: (0, j)),
                  pl.BlockSpec((1, tn), lambda j: (0, j))],
        out_specs=pl.BlockSpec((SUBLANES, tn), lambda j: (0, j)),
        out_shape=jax.ShapeDtypeStruct((SUBLANES, n), F32),
        compiler_params=_cparams(("arbitrary",)),
        name="ada",
    )(c8, w, b)


def _modulated_norm(x, nw, sc, sh):
    r = x * lax.rsqrt(jnp.mean(x * x, axis=-1, keepdims=True) + EPS) * nw
    return r * (1.0 + sc) + sh


def _norm_proj_kernel(x_ref, nw_ref, sc_ref, sh_ref, w_ref, o_ref, xn_ref, *, precision):
    @pl.when(pl.program_id(2) == 0)
    def _():
        xn = _modulated_norm(x_ref[0], nw_ref[...], sc_ref[0], sh_ref[0])
        xn_ref[...] = xn.astype(xn_ref.dtype)

    o_ref[0] = jnp.dot(xn_ref[...], w_ref[...], preferred_element_type=F32,
                       precision=precision).astype(o_ref.dtype)


def _norm_proj(x, nw, sc, sh, w, out_dtype, tm, tn, name, precision=None):
    b, s, d = x.shape
    n = w.shape[1]
    tm = min(tm, s)
    tn = min(tn, n)
    return pl.pallas_call(
        functools.partial(_norm_proj_kernel, precision=precision),
        grid=(b, s // tm, n // tn),
        in_specs=[pl.BlockSpec((1, tm, d), lambda bi, m, j: (bi, m, 0)),
                  pl.BlockSpec((1, d), lambda bi, m, j: (0, 0)),
                  pl.BlockSpec((1, 1, d), lambda bi, m, j: (bi, 0, 0)),
                  pl.BlockSpec((1, 1, d), lambda bi, m, j: (bi, 0, 0)),
                  pl.BlockSpec((d, tn), lambda bi, m, j: (0, j))],
        out_specs=pl.BlockSpec((1, tm, tn), lambda bi, m, j: (bi, m, j)),
        out_shape=jax.ShapeDtypeStruct((b, s, n), out_dtype),
        scratch_shapes=[pltpu.VMEM((tm, d), w.dtype)],
        compiler_params=_cparams(("parallel", "parallel", "arbitrary")),
        name=name,
    )(x, nw, sc, sh, w)


def _norm_proj_t_kernel(x_ref, nw_ref, sc_ref, sh_ref, wt_ref, rs_ref, o_ref, xn_ref):
    @pl.when(pl.program_id(2) == 0)
    def _():
        xn = _modulated_norm(x_ref[0], nw_ref[...], sc_ref[0], sh_ref[0])
        xn_ref[...] = xn.astype(xn_ref.dtype)

    o_ref[0] = (_dot_nt(wt_ref[...], xn_ref[...]) * rs_ref[...]).astype(o_ref.dtype)


def _norm_proj_t(x, nw, sc, sh, wt, row_scale, out_dtype, tm, tn, name):
    b, s, d = x.shape
    n = wt.shape[0]
    tm = min(tm, s)
    tn = min(tn, n)
    return pl.pallas_call(
        _norm_proj_t_kernel,
        grid=(b, s // tm, n // tn),
        in_specs=[pl.BlockSpec((1, tm, d), lambda bi, m, j: (bi, m, 0)),
                  pl.BlockSpec((1, d), lambda bi, m, j: (0, 0)),
                  pl.BlockSpec((1, 1, d), lambda bi, m, j: (bi, 0, 0)),
                  pl.BlockSpec((1, 1, d), lambda bi, m, j: (bi, 0, 0)),
                  pl.BlockSpec((tn, d), lambda bi, m, j: (j, 0)),
                  pl.BlockSpec((tn, 1), lambda bi, m, j: (j, 0))],
        out_specs=pl.BlockSpec((1, tn, tm), lambda bi, m, j: (bi, j, m)),
        out_shape=jax.ShapeDtypeStruct((b, n, s), out_dtype),
        scratch_shapes=[pltpu.VMEM((tm, d), wt.dtype)],
        compiler_params=_cparams(("parallel", "parallel", "arbitrary")),
        name=name,
    )(x, nw, sc, sh, wt, row_scale)


CONV_HALO = SUBLANES
CONV_COLS = 512
GROUP_COLS = D_INNER // SSD_GROUPS
HEADS_PER_GROUP = SSD_HEADS // SSD_GROUPS
PF_COLS = CONV_CH + D_INNER + 2 * D_MODEL
PF_Z_BLOCK = CONV_CH // D_INNER
PF_GS_BLOCK = (CONV_CH + D_INNER) // D_MODEL
PF_GA_BLOCK = PF_GS_BLOCK + 1


def _softplus(x):
    return jnp.maximum(x, 0.0) + jnp.log(1.0 + jnp.exp(-jnp.abs(x)))


def _ssd_kernel(xbc_ref, z_ref, dt_ref, cw_ref, cb_ref, dtb_ref, alog_ref, dsk_ref, nw_ref,
                o_ref, state_ref, win_ref, act_ref):
    L = CHUNK

    @pl.when(pl.program_id(1) == 0)
    def _():
        state_ref[...] = jnp.zeros_like(state_ref)
        win_ref[0:CONV_HALO, :] = jnp.zeros((CONV_HALO, CONV_CH), F32)

    win_ref[CONV_HALO:CONV_HALO + L, :] = xbc_ref[0]
    base = CONV_HALO - (CONV_K - 1)
    for j in range(CONV_CH // CONV_COLS):
        cs = slice(j * CONV_COLS, (j + 1) * CONV_COLS)
        acc = cb_ref[:, cs] + cw_ref[0:1, cs] * win_ref[base:base + L, cs]
        for k in range(1, CONV_K):
            acc = acc + cw_ref[k:k + 1, cs] * win_ref[base + k:base + k + L, cs]
        act_ref[:, cs] = _silu(acc)
    win_ref[0:CONV_HALO, :] = win_ref[L:L + CONV_HALO, :]

    dtv = _softplus(dt_ref[0] + dtb_ref[...])
    a = -jnp.exp(alog_ref[...])
    row = lax.broadcasted_iota(I32, (L, L), 0)
    col = lax.broadcasted_iota(I32, (L, L), 1)
    tril = row >= col
    a_cum = jnp.dot(tril.astype(F32), dtv * a, preferred_element_type=F32,
                    precision=HIGHEST)
    a_cum_t = a_cum.T
    dt_t = dtv.T
    a_last = a_cum[L - 1:L, :]
    to_end = jnp.exp(a_last - a_cum) * dtv
    ea = jnp.exp(a_cum)
    cdec = jnp.exp(a_last)

    head_of_col = lax.broadcasted_iota(I32, (L, GROUP_COLS), 1) // SSD_HEAD_DIM
    neg_inf = jnp.float32(-jnp.inf)

    for g in range(SSD_GROUPS):
        xs = slice(g * GROUP_COLS, (g + 1) * GROUP_COLS)
        bs = slice(D_INNER + g * SSD_STATE, D_INNER + (g + 1) * SSD_STATE)
        cs = slice(D_INNER + (SSD_GROUPS + g) * SSD_STATE,
                   D_INNER + (SSD_GROUPS + g + 1) * SSD_STATE)
        xg = act_ref[:, xs]
        bg = act_ref[:, bs]
        xb = xg.astype(BF16)
        bb = bg.astype(BF16)
        cb16 = act_ref[:, cs].astype(BF16)
        cbm = _dot_nt(cb16, bb)
        prev_t = state_ref[:, xs]
        y_off = jnp.dot(cb16, prev_t.astype(BF16), preferred_element_type=F32)

        y = jnp.zeros((L, GROUP_COLS), F32)
        ea_g = jnp.zeros((L, GROUP_COLS), F32)
        te_g = jnp.zeros((L, GROUP_COLS), F32)
        cd_g = jnp.zeros((1, GROUP_COLS), F32)
        for r in range(HEADS_PER_GROUP):
            h = g * HEADS_PER_GROUP + r
            seg = a_cum[:, h:h + 1] - a_cum_t[h:h + 1, :]
            decay = jnp.exp(jnp.where(tril, seg, neg_inf))
            wts = (cbm * decay * dt_t[h:h + 1, :]).astype(BF16)
            yd = jnp.dot(wts, xb, preferred_element_type=F32)
            sel = head_of_col == r
            y = jnp.where(sel, yd, y)
            ea_g = jnp.where(sel, ea[:, h:h + 1], ea_g)
            te_g = jnp.where(sel, to_end[:, h:h + 1], te_g)
            cd_g = jnp.where(sel[0:1, :], cdec[:, h:h + 1], cd_g)
        y = y + y_off * ea_g + dsk_ref[:, xs] * xg
        xs_scaled = (xg * te_g).astype(BF16)
        st = jnp.dot(bg.T.astype(BF16), xs_scaled, preferred_element_type=F32)
        state_ref[:, xs] = prev_t * cd_g + st

        yz = y * _silu(z_ref[0, :, xs])
        ms = jnp.mean(yz * yz, axis=-1, keepdims=True)
        o_ref[0, :, xs] = (yz * lax.rsqrt(ms + EPS) * nw_ref[:, xs]).astype(o_ref.dtype)


def _ssd(pf, dtraw, cw, cb, dtb, alog, dsk, nw):
    b, s, _ = pf.shape
    nc = s // CHUNK
    const2 = lambda bi, c: (0, 0)
    return pl.pallas_call(
        _ssd_kernel,
        grid=(b, nc),
        in_specs=[pl.BlockSpec((1, CHUNK, CONV_CH), lambda bi, c: (bi, c, 0)),
                  pl.BlockSpec((1, CHUNK, D_INNER), lambda bi, c: (bi, c, PF_Z_BLOCK)),
                  pl.BlockSpec((1, CHUNK, LANES), lambda bi, c: (bi, c, 0)),
                  pl.BlockSpec((CONV_K, CONV_CH), const2),
                  pl.BlockSpec((1, CONV_CH), const2),
                  pl.BlockSpec((1, LANES), const2),
                  pl.BlockSpec((1, LANES), const2),
                  pl.BlockSpec((1, D_INNER), const2),
                  pl.BlockSpec((1, D_INNER), const2)],
        out_specs=pl.BlockSpec((1, CHUNK, D_INNER), lambda bi, c: (bi, c, 0)),
        out_shape=jax.ShapeDtypeStruct((b, s, D_INNER), BF16),
        scratch_shapes=[pltpu.VMEM((SSD_STATE, D_INNER), F32),
                        pltpu.VMEM((CONV_HALO + CHUNK, CONV_CH), F32),
                        pltpu.VMEM((CHUNK, CONV_CH), F32)],
        compiler_params=_cparams(("parallel", "arbitrary")),
        name="ssd",
    )(pf, pf, dtraw, cw, cb, dtb, alog, dsk, nw)


ATT_TILE = 512
ATT_HEADS_PER_STEP = 2
LOG2E = math.log2(math.e)
BF16_ROWS = 2 * SUBLANES
ACC_ROWS = ATTN_V_DIM + BF16_ROWS


def _t5_bias_lut(rb_row):
    n = lax.broadcasted_iota(I32, (LANES, LANES), 1)
    bkt = lax.broadcasted_iota(I32, (LANES, LANES), 0)
    max_exact = NUM_BUCKETS // 2
    nf = jnp.maximum(n, 1).astype(F32)
    large = max_exact + (jnp.log(nf / max_exact) / math.log(MAX_DISTANCE / max_exact)
                         * (NUM_BUCKETS - max_exact)).astype(I32)
    large = jnp.minimum(large, NUM_BUCKETS - 1)
    bucket = jnp.where(n < max_exact, n, large)
    onehot = (bucket == bkt).astype(F32)
    lut = jnp.dot(jnp.broadcast_to(rb_row, (SUBLANES, LANES)), onehot,
                  preferred_element_type=F32, precision=HIGHEST)
    return lut[0:1, :]


def _attn_kernel(pmin_ref, pmax_ref, qt_ref, k_ref, vt_ref, pos_ref, rb_ref,
                 lamv_ref, nw_ref, o_ref, qm_ref, m_ref, acc_ref, sa_ref, sb_ref, *, tile, nh):
    T = tile
    HW = ATTN_V_DIM
    bi = pl.program_id(0)
    qi = pl.program_id(2)
    neg_inf = jnp.float32(-jnp.inf)
    nck = T // LANES

    feat = lax.broadcasted_iota(I32, (HW, T), 0)
    for hh in range(nh):
        qt = qt_ref[0, hh * HW:(hh + 1) * HW, :]
        zero = jnp.zeros_like(qt)
        qm_ref[hh, 0] = jnp.where(feat < ATTN_HEAD_DIM, qt, zero)
        qm_ref[hh, 1] = jnp.where(feat >= ATTN_HEAD_DIM, qt, zero)
    m_ref[...] = jnp.full_like(m_ref, neg_inf)
    acc_ref[...] = jnp.zeros_like(acc_ref)
    ones_rows = (lax.broadcasted_iota(I32, (BF16_ROWS, T), 0) == 0).astype(BF16)

    luts = [_t5_bias_lut(rb_ref[hh]) * LOG2E for hh in range(nh)]
    c_far = [lut[:, LANES - 1:LANES] for lut in luts]
    lut_b = [jnp.broadcast_to(lut, (LANES, LANES)) for lut in luts]
    qoff = pl.multiple_of(qi * T, T)
    pos_q = pos_ref[0, :, pl.ds(qoff, T)]

    def scores(ki, dst_ref):
        koff = pl.multiple_of(ki * T, T)
        for hh in range(nh):
            k = k_ref[0, pl.ds(koff, T), hh * HW:(hh + 1) * HW]
            for i in range(2):
                dst_ref[hh, i] = jnp.dot(k, qm_ref[hh, i], preferred_element_type=F32)

    def consume(ki, src_ref, near, diag):
        koff = pl.multiple_of(ki * T, T)
        if near:
            dists = []
            for kc in range(nck):
                pk_row = pos_ref[0, :, pl.ds(koff + kc * LANES, LANES)]
                pk = jnp.broadcast_to(pk_row, (LANES, LANES)).T
                dists.append([jnp.clip(pos_q[:, qc * LANES:(qc + 1) * LANES] - pk, 0, LANES - 1)
                              for qc in range(nck)])
            if diag:
                kr = lax.broadcasted_iota(I32, (T, T), 0)
                qc_ = lax.broadcasted_iota(I32, (T, T), 1)
                causal = kr <= qc_
        for hh in range(nh):
            vt = jnp.concatenate([vt_ref[0, hh * HW:(hh + 1) * HW, pl.ds(koff, T)], ones_rows],
                                 axis=0)
            if near:
                bias = jnp.concatenate(
                    [jnp.concatenate([jnp.take_along_axis(lut_b[hh], d, axis=1) for d in row],
                                     axis=1) for row in dists], axis=0)
                if diag:
                    bias = jnp.where(causal, bias, neg_inf)
            for i in range(2):
                s = src_ref[hh, i]
                if near:
                    s = s + bias
                    m_tile = jnp.max(s, axis=0, keepdims=True)
                else:
                    m_tile = jnp.max(s, axis=0, keepdims=True) + c_far[hh]
                m_prev = m_ref[hh, i]
                m_new = jnp.maximum(m_prev, m_tile)
                shift = m_new if near else m_new - c_far[hh]
                p = jnp.exp2(s - shift)
                alpha = jnp.exp2(m_prev - m_new)
                acc_ref[hh, i] = alpha * acc_ref[hh, i] + jnp.dot(
                    vt, p.astype(vt.dtype), preferred_element_type=F32)
                m_ref[hh, i] = m_new

    def step(t, src_ref, dst_ref):
        near = pmin_ref[bi, qi] - pmax_ref[bi, t] < N_FAR

        @pl.when(near)
        def _():
            scores(t + 1, dst_ref)
            consume(t, src_ref, True, False)

        @pl.when(jnp.logical_not(near))
        def _():
            scores(t + 1, dst_ref)
            consume(t, src_ref, False, False)

    def body(jj, carry):
        step(2 * jj, sa_ref, sb_ref)

        @pl.when(2 * jj + 1 < qi)
        def _():
            step(2 * jj + 1, sb_ref, sa_ref)

        return carry

    scores(0, sa_ref)
    lax.fori_loop(0, (qi + 1) // 2, body, 0)

    @pl.when(qi % 2 == 0)
    def _():
        consume(qi, sa_ref, True, True)

    @pl.when(qi % 2 == 1)
    def _():
        consume(qi, sb_ref, True, True)

    lv = lamv_ref[...]
    lam = (jnp.exp(jnp.sum(lv[0:1, :] * lv[1:2, :], axis=-1, keepdims=True))
           - jnp.exp(jnp.sum(lv[2:3, :] * lv[3:4, :], axis=-1, keepdims=True))
           + LAMBDA_INIT)
    nv = ATTN_V_DIM
    for hh in range(nh):
        att = (acc_ref[hh, 0, 0:nv, :] / acc_ref[hh, 0, nv:nv + 1, :]
               - lam * (acc_ref[hh, 1, 0:nv, :] / acc_ref[hh, 1, nv:nv + 1, :]))
        ms = jnp.mean(att * att, axis=0, keepdims=True)
        y = att * lax.rsqrt(ms + EPS) * nw_ref[...] * (1.0 - LAMBDA_INIT)
        o_ref[0, :, hh * HW:(hh + 1) * HW] = y.T.astype(o_ref.dtype)


def _attn(qvt, kk, positions, rel_bias, lamv, nw_col):
    b, s, _ = kk.shape
    T = min(ATT_TILE, s)
    nq = s // T
    nh = ATT_HEADS_PER_STEP
    H = ATTN_HEADS // nh
    W = nh * ATTN_V_DIM
    pos_blk = positions.reshape(b, nq, T)
    pmin = jnp.min(pos_blk, axis=-1)
    pmax = jnp.max(pos_blk, axis=-1)
    posr = positions.reshape(b, 1, s)
    rb = jnp.zeros((ATTN_HEADS, 1, LANES), F32).at[:, 0, :NUM_BUCKETS].set(rel_bias.T)
    grid_spec = pltpu.PrefetchScalarGridSpec(
        num_scalar_prefetch=2,
        grid=(b, H, nq),
        in_specs=[pl.BlockSpec((1, W, T), lambda bi, h, qi, *_: (bi, h, qi)),
                  pl.BlockSpec((1, s, W), lambda bi, h, qi, *_: (bi, 0, h)),
                  pl.BlockSpec((1, W, s), lambda bi, h, qi, *_: (bi, H + h, 0)),
                  pl.BlockSpec((1, 1, s), lambda bi, h, qi, *_: (bi, 0, 0)),
                  pl.BlockSpec((nh, 1, LANES), lambda bi, h, qi, *_: (h, 0, 0)),
                  pl.BlockSpec((SUBLANES, LANES), lambda bi, h, qi, *_: (0, 0)),
                  pl.BlockSpec((LANES, 1), lambda bi, h, qi, *_: (0, 0))],
        out_specs=pl.BlockSpec((1, T, W), lambda bi, h, qi, *_: (bi, qi, h)),
        scratch_shapes=[pltpu.VMEM((nh, 2, LANES, T), BF16),
                        pltpu.VMEM((nh, 2, 1, T), F32),
                        pltpu.VMEM((nh, 2, ACC_ROWS, T), F32),
                        pltpu.VMEM((nh, 2, T, T), F32),
                        pltpu.VMEM((nh, 2, T, T), F32)])
    return pl.pallas_call(
        functools.partial(_attn_kernel, tile=T, nh=nh),
        grid_spec=grid_spec,
        out_shape=jax.ShapeDtypeStruct((b, s, ATTN_HEADS * ATTN_V_DIM), BF16),
        compiler_params=_cparams(("parallel", "parallel", "arbitrary")),
        name="attn",
    )(pmin, pmax, qvt, kk, qvt, posr, rb, lamv, nw_col)


def _merge_kernel(x_ref, ys_ref, ya_ref, gs_ref, ga_ref, g1_ref, ws_ref, wa_ref, wo_ref, o_ref):
    y_ssd = jnp.dot(ys_ref[0], ws_ref[...], preferred_element_type=F32)
    y_att = jnp.dot(ya_ref[0], wa_ref[...], preferred_element_type=F32)
    mix_in = _sigmoid(gs_ref[0]) * y_ssd + _sigmoid(ga_ref[0]) * y_att
    mix = jnp.dot(mix_in.astype(BF16), wo_ref[...], preferred_element_type=F32)
    o_ref[0] = x_ref[0] + g1_ref[0] * mix


def _merge(x, ys, ya, pf, g1, ws, wa, wo, tm=512):
    b, s, d = x.shape
    tm = min(tm, s)
    const2 = lambda bi, m: (0, 0)
    return pl.pallas_call(
        _merge_kernel,
        grid=(b, s // tm),
        in_specs=[pl.BlockSpec((1, tm, d), lambda bi, m: (bi, m, 0)),
                  pl.BlockSpec((1, tm, D_INNER), lambda bi, m: (bi, m, 0)),
                  pl.BlockSpec((1, tm, d), lambda bi, m: (bi, m, 0)),
                  pl.BlockSpec((1, tm, d), lambda bi, m: (bi, m, PF_GS_BLOCK)),
                  pl.BlockSpec((1, tm, d), lambda bi, m: (bi, m, PF_GA_BLOCK)),
                  pl.BlockSpec((1, 1, d), lambda bi, m: (bi, 0, 0)),
                  pl.BlockSpec((D_INNER, d), const2),
                  pl.BlockSpec((d, d), const2),
                  pl.BlockSpec((d, d), const2)],
        out_specs=pl.BlockSpec((1, tm, d), lambda bi, m: (bi, m, 0)),
        out_shape=jax.ShapeDtypeStruct((b, s, d), F32),
        compiler_params=_cparams(("parallel", "parallel")),
        name="merge",
    )(x, ys, ya, pf, pf, g1, ws, wa, wo)


TOPK_TOKENS = 256
N_INST = 2 * PEER_HEADS
PAIR_CANDS = [(a, c) for a in range(PEER_TOPK) for c in range(PEER_TOPK)
              if (a + 1) * (c + 1) <= PEER_TOPK]


def _topk_kernel(q_ref, kp_ref, e1_ref, e2_ref, g_ref, s_ref, v_ref, i_ref, cs_ref, ce_ref,
                 os_ref, oe_ref):
    tn = q_ref.shape[0]
    neg_inf = jnp.float32(-jnp.inf)
    s_ref[...] = _dot_nt(kp_ref[...], q_ref[...].astype(BF16))

    def level1(it, prev):
        best = jnp.full((N_INST, tn), neg_inf, F32)
        bidx = jnp.zeros((N_INST, tn), I32)
        for key in range(N_KEYS):
            rows = slice(key * N_INST, (key + 1) * N_INST)
            sk = jnp.where(prev == key, neg_inf, s_ref[rows, :])
            s_ref[rows, :] = sk
            upd = sk > best
            best = jnp.where(upd, sk, best)
            bidx = jnp.where(upd, key, bidx)
        v_ref[it] = best
        i_ref[it] = bidx
        return bidx

    lax.fori_loop(0, PEER_TOPK, level1, jnp.full((N_INST, tn), -1, I32))

    H = PEER_HEADS
    for ci, (a, c) in enumerate(PAIR_CANDS):
        cs_ref[ci] = v_ref[a, 0:H, :] + v_ref[c, H:2 * H, :]
        ce_ref[ci] = i_ref[a, 0:H, :] * N_KEYS + i_ref[c, H:2 * H, :]

    def level2(it, prev):
        best = jnp.full((H, tn), neg_inf, F32)
        bci = jnp.zeros((H, tn), I32)
        bex = jnp.zeros((H, tn), I32)
        for ci in range(len(PAIR_CANDS)):
            sk = jnp.where(prev == ci, neg_inf, cs_ref[ci])
            cs_ref[ci] = sk
            upd = sk > best
            best = jnp.where(upd, sk, best)
            bci = jnp.where(upd, ci, bci)
            bex = jnp.where(upd, ce_ref[ci], bex)
        os_ref[it] = best
        oe_ref[it] = bex
        return bci

    lax.fori_loop(0, PEER_TOPK, level2, jnp.full((H, tn), -1, I32))

    sc = os_ref[...]
    ex = jnp.exp(sc - sc[0:1])
    gate = ex / jnp.sum(ex, axis=0, keepdims=True)
    ex_i = oe_ref[...]
    rows = PEER_TOPK * H
    g_ref[...] = gate.reshape(rows, tn).T
    key_bits = N_KEYS.bit_length() - 1
    e1 = lax.shift_right_logical(ex_i, key_bits).astype(F32).reshape(rows, tn)
    e2 = lax.bitwise_and(ex_i, N_KEYS - 1).astype(F32).reshape(rows, tn)
    e1_ref[...] = e1.T.astype(I32)
    e2_ref[...] = e2.T.astype(I32)


def _peer_topk(q, kperm):
    t, d = q.shape
    tn = min(TOPK_TOKENS, t)
    lanes = PEER_TOPK * PEER_HEADS
    ncand = len(PAIR_CANDS)
    out = jax.ShapeDtypeStruct((t, lanes), I32)
    return pl.pallas_call(
        _topk_kernel,
        grid=(t // tn,),
        in_specs=[pl.BlockSpec((tn, d), lambda i: (i, 0)),
                  pl.BlockSpec(kperm.shape, lambda i: (0, 0))],
        out_specs=[pl.BlockSpec((tn, lanes), lambda i: (i, 0))] * 3,
        out_shape=[out, out, jax.ShapeDtypeStruct((t, lanes), F32)],
        scratch_shapes=[pltpu.VMEM((N_KEYS * N_INST, tn), F32),
                        pltpu.VMEM((PEER_TOPK, N_INST, tn), F32),
                        pltpu.VMEM((PEER_TOPK, N_INST, tn), I32),
                        pltpu.VMEM((ncand, PEER_HEADS, tn), F32),
                        pltpu.VMEM((ncand, PEER_HEADS, tn), I32),
                        pltpu.VMEM((PEER_TOPK, PEER_HEADS, tn), F32),
                        pltpu.VMEM((PEER_TOPK, PEER_HEADS, tn), I32)],
        compiler_params=_cparams(("parallel",)),
        name="peer_topk",
    )(q, kperm)


MIX_TOKENS = 512
MIX_EXPERTS = 1024
MIX_SUB = 2 * N_KEYS
G_HALF = N_KEYS // 2
G_PITCH = G_HALF + SUBLANES
G_BUILD_UNROLL = 16


def _gelu_exact(x):
    return 0.5 * x * (1.0 + lax.erf(x * (1.0 / math.sqrt(2.0))))


def _mix_kernel(h_ref, nw_ref, sc_ref, sh_ref, g2_ref, fw_ref, e1_ref, e2_ref, gt_ref,
                ut_ref, v_ref, o_ref, xb_ref, g_ref, acc_ref):
    tb = h_ref.shape[1]
    c = pl.program_id(2)
    hi_mask = jnp.uint32(0xFFFF0000)

    @pl.when(c == 0)
    def _():
        xn = _modulated_norm(h_ref[0], nw_ref[...], sc_ref[0], sh_ref[0])
        xb_ref[...] = xn.astype(BF16)
        acc_ref[...] = jnp.zeros_like(acc_ref)
        sub = lax.broadcasted_iota(I32, (N_KEYS, LANES), 0)
        row_key = jnp.where(sub < G_HALF, 2 * sub, 2 * (sub - G_HALF) + 1)

        def build(t, carry):
            eq1 = row_key == e1_ref[pl.ds(t, 1), :]
            eq2 = sub == e2_ref[pl.ds(t, 1), :]
            a = jnp.where(eq1, gt_ref[pl.ds(t, 1), :], 0.0).astype(BF16)
            bm = jnp.where(eq2, 1.0, 0.0).astype(BF16)
            gm = _dot_nt(a, bm)
            top = lax.bitcast_convert_type(gm[0:G_HALF].astype(BF16).astype(F32), U32)
            bot = lax.bitcast_convert_type(gm[G_HALF:N_KEYS].astype(BF16).astype(F32), U32)
            g_ref[pl.ds(pl.multiple_of(t * G_PITCH, SUBLANES), G_HALF), :] = (
                top | lax.shift_right_logical(bot, jnp.uint32(16)))
            return carry

        lax.fori_loop(0, tb, build, 0, unroll=G_BUILD_UNROLL)

    xb = xb_ref[...]
    nsub = MIX_EXPERTS // MIX_SUB

    def hidden(j):
        return jnp.dot(xb, ut_ref[:, j * MIX_SUB:(j + 1) * MIX_SUB], preferred_element_type=F32)

    hid = hidden(0)
    acc = acc_ref[...]
    for j in range(nsub):
        nxt = hidden(j + 1) if j + 1 < nsub else None
        words = g_ref[pl.ds(c * nsub + j, tb, stride=G_PITCH), :]
        gates = jnp.concatenate(
            [lax.bitcast_convert_type(words & hi_mask, F32),
             lax.bitcast_convert_type(lax.shift_left(words, jnp.uint32(16)), F32)], axis=1)
        p = (_gelu_exact(hid) * gates).astype(BF16)
        acc = acc + jnp.dot(p, v_ref[j * MIX_SUB:(j + 1) * MIX_SUB, :],
                            preferred_element_type=F32)
        hid = nxt
    acc_ref[...] = acc

    @pl.when(c == pl.num_programs(2) - 1)
    def _():
        h2 = h_ref[0] + g2_ref[0] * acc_ref[...]
        r = h2 * lax.rsqrt(jnp.mean(h2 * h2, axis=-1, keepdims=True) + EPS)
        o_ref[0] = r * fw_ref[...]


def _peer_mix(h, nw, sc, sh, g2, fw, e1, e2, gate, ut, v):
    b, s, d = h.shape
    tb = min(MIX_TOKENS, s)
    nb = s // tb
    ne = v.shape[0] // MIX_EXPERTS
    lanes = PEER_TOPK * PEER_HEADS
    tok = lambda bi, m, c: (bi * nb + m, 0)
    vec = lambda bi, m, c: (bi, 0, 0)
    return pl.pallas_call(
        _mix_kernel,
        grid=(b, nb, ne),
        in_specs=[pl.BlockSpec((1, tb, d), lambda bi, m, c: (bi, m, 0)),
                  pl.BlockSpec((1, d), lambda bi, m, c: (0, 0)),
                  pl.BlockSpec((1, 1, d), vec),
                  pl.BlockSpec((1, 1, d), vec),
                  pl.BlockSpec((1, 1, d), vec),
                  pl.BlockSpec((1, d), lambda bi, m, c: (0, 0)),
                  pl.BlockSpec((tb, lanes), tok),
                  pl.BlockSpec((tb, lanes), tok),
                  pl.BlockSpec((tb, lanes), tok),
                  pl.BlockSpec((d, MIX_EXPERTS), lambda bi, m, c: (0, c)),
                  pl.BlockSpec((MIX_EXPERTS, d), lambda bi, m, c: (c, 0))],
        out_specs=pl.BlockSpec((1, tb, d), lambda bi, m, c: (bi, m, 0)),
        out_shape=jax.ShapeDtypeStruct((b, s, d), F32),
        scratch_shapes=[pltpu.VMEM((tb, d), BF16),
                        pltpu.VMEM((tb * G_PITCH, LANES), U32),
                        pltpu.VMEM((tb, d), F32)],
        compiler_params=_cparams(("parallel", "parallel", "arbitrary")),
        name="peer_mix",
    )(h, nw, sc, sh, g2, fw, e1, e2, gate, ut, v)


def _pad_lanes(vec, width=LANES):
    return jnp.zeros((1, width), F32).at[0, :vec.shape[0]].set(vec)


def kernel(x, c, positions, ada_w, ada_b, norm1_w, norm2_w, final_norm_w, w_in, conv_w, conv_b,
           dt_bias, a_log, d_skip, ssd_norm_w, w_ssd_o, lambda_q1, lambda_k1, lambda_q2,
           lambda_k2, attn_norm_w, w_attn_o, rel_bias, w_out, peer_wq, peer_key1, peer_key2,
           peer_u, peer_v):
    b, s, d = x.shape
    l = 0

    c8 = jnp.zeros((SUBLANES, d), F32).at[:b].set(c)
    ada = _ada(c8, ada_w[l], ada_b[l][None, :])[:b]
    sh1, sc1, g1, sh2, sc2, g2 = [m[:, None, :] for m in jnp.split(ada, 6, axis=-1)]

    wl = w_in[l]
    w_f = jnp.concatenate([wl[:, OFF_XBC:OFF_DT], wl[:, OFF_Z:OFF_XBC], wl[:, OFF_GS:IN_COLS]],
                          axis=1).astype(BF16)
    w_k = wl[:, OFF_K:OFF_V].astype(BF16)
    w_qv_t = jnp.concatenate([wl[:, OFF_Q:OFF_K], wl[:, OFF_V:OFF_GS]], axis=1).T.astype(BF16)
    w_dt = jnp.zeros((d, LANES), F32).at[:, :SSD_HEADS].set(wl[:, OFF_DT:OFF_Q])
    nw1 = norm1_w[l][None, :]

    pf = _norm_proj(x, nw1, sc1, sh1, w_f, F32, 1024, 1024, "proj_f32")
    kk = _norm_proj(x, nw1, sc1, sh1, w_k, BF16, 512, 1024, "proj_k")
    nqk = ATTN_HEADS * 2 * ATTN_HEAD_DIM
    row_scale = jnp.concatenate([jnp.full((nqk, 1), ATTN_HEAD_DIM ** -0.5 * LOG2E, F32),
                                 jnp.ones((ATTN_HEADS * ATTN_V_DIM, 1), F32)])
    qvt = _norm_proj_t(x, nw1, sc1, sh1, w_qv_t, row_scale, BF16, 512, 1024, "proj_qvt")
    dtraw = _norm_proj(x, nw1, sc1, sh1, w_dt, F32, 512, LANES, "proj_dt", precision=HIGHEST)

    dsk = jnp.repeat(d_skip[l], SSD_HEAD_DIM)[None, :]
    yn = _ssd(pf, dtraw, conv_w[l], conv_b[l][None, :], _pad_lanes(dt_bias[l]),
              _pad_lanes(a_log[l]), dsk, ssd_norm_w[l][None, :])

    lamv = jnp.zeros((SUBLANES, LANES), F32)
    for i, v_ in enumerate((lambda_q1, lambda_k1, lambda_q2, lambda_k2)):
        lamv = lamv.at[i, :ATTN_HEAD_DIM].set(v_[l])
    an = _attn(qvt, kk, positions, rel_bias, lamv, attn_norm_w[l][:, None])

    h1 = _merge(x, yn, an, pf, g1, w_ssd_o[l].astype(BF16), w_attn_o[l].astype(BF16),
                w_out[l].astype(BF16))

    nw2 = norm2_w[l][None, :]
    q = _norm_proj(h1, nw2, sc2, sh2, peer_wq[l].astype(BF16), F32, 512, 1024, "peer_q")
    keys = jnp.stack([peer_key1[l], peer_key2[l]])
    kperm = jnp.einsum('ahkd,ab,hg->kahgbd', keys, jnp.eye(2, dtype=F32),
                       jnp.eye(PEER_HEADS, dtype=F32))
    kperm = kperm.reshape(N_KEYS * N_INST, PEER_HEADS * PEER_DK).astype(BF16)
    e1, e2, gate = _peer_topk(q.reshape(b * s, PEER_HEADS * PEER_DK), kperm)
    return _peer_mix(h1, nw2, sc2, sh2, g2, final_norm_w[None, :], e1, e2, gate,
                     peer_u[l].T.astype(BF16), peer_v[l].astype(BF16))
```

```python
import functools
import math

import jax
import jax.numpy as jnp
from jax import lax
from jax.experimental import pallas as pl
from jax.experimental.pallas import tpu as pltpu

F32 = jnp.float32
BF16 = jnp.bfloat16
I32 = jnp.int32
U32 = jnp.uint32
HIGHEST = lax.Precision.HIGHEST

D_MODEL = 1024
D_INNER = 2048
SSD_HEADS = 32
SSD_HEAD_DIM = 64
SSD_GROUPS = 8
SSD_STATE = 128
CONV_K = 4
CONV_CH = D_INNER + 2 * SSD_GROUPS * SSD_STATE
CHUNK = 128
ATTN_HEADS = 8
ATTN_HEAD_DIM = 64
ATTN_V_DIM = 2 * ATTN_HEAD_DIM
NUM_BUCKETS = 32
MAX_DISTANCE = 128
PEER_HEADS = 8
N_KEYS = 128
N_EXPERTS = N_KEYS * N_KEYS
PEER_DK = 128
PEER_TOPK = 16
EPS = 1e-6
LAMBDA_INIT = 0.8 - 0.6 * math.exp(0.0)

LANES = 128
SUBLANES = 8
VMEM_LIMIT = 48 * 1024 * 1024

OFF_Z = 0
OFF_XBC = OFF_Z + D_INNER
OFF_DT = OFF_XBC + CONV_CH
OFF_Q = OFF_DT + SSD_HEADS
OFF_K = OFF_Q + ATTN_HEADS * 2 * ATTN_HEAD_DIM
OFF_V = OFF_K + ATTN_HEADS * 2 * ATTN_HEAD_DIM
OFF_GS = OFF_V + ATTN_HEADS * ATTN_V_DIM
OFF_GA = OFF_GS + D_MODEL
IN_COLS = OFF_GA + D_MODEL


def _t5_far_distance():
    max_exact = NUM_BUCKETS // 2
    n = max_exact
    while True:
        b = max_exact + int(math.log(n / max_exact) / math.log(MAX_DISTANCE / max_exact)
                            * (NUM_BUCKETS - max_exact))
        if b >= NUM_BUCKETS - 1:
            return n
        n += 1


N_FAR = _t5_far_distance()
assert N_FAR <= LANES


def _sigmoid(x):
    return 1.0 / (1.0 + jnp.exp(-x))


def _silu(x):
    return x * _sigmoid(x)


def _cparams(sem, vmem_limit=VMEM_LIMIT):
    return pltpu.CompilerParams(dimension_semantics=sem, vmem_limit_bytes=vmem_limit)


def _dot_nt(a, b):
    return lax.dot_general(a, b, (((1,), (1,)), ((), ())), preferred_element_type=F32)


def _ada_kernel(c_ref, w_ref, b_ref, o_ref):
    o_ref[...] = jnp.dot(_silu(c_ref[...]), w_ref[...], preferred_element_type=F32,
                         precision=HIGHEST) + b_ref[...]


def _ada(c8, w, b):
    n = w.shape[1]
    tn = D_MODEL
    return pl.pallas_call(
        _ada_kernel,
        grid=(n // tn,),
        in_specs=[pl.BlockSpec((SUBLANES, D_MODEL), lambda j: (0, 0)),
                  pl.BlockSpec((D_MODEL, tn), lambda j: (0, j)),
                  pl.BlockSpec((1, tn), lambda j: (0, j))],
        out_specs=pl.BlockSpec((SUBLANES, tn), lambda j: (0, j)),
        out_shape=jax.ShapeDtypeStruct((SUBLANES, n), F32),
        compiler_params=_cparams(("arbitrary",)),
        name="ada",
    )(c8, w, b)


def _modulated_norm(x, nw, sc, sh):
    r = x * lax.rsqrt(jnp.mean(x * x, axis=-1, keepdims=True) + EPS) * nw
    return r * (1.0 + sc) + sh


def _norm_proj_kernel(x_ref, nw_ref, sc_ref, sh_ref, w_ref, o_ref, xn_ref, *, precision):
    @pl.when(pl.program_id(2) == 0)
    def _():
        xn = _modulated_norm(x_ref[0], nw_ref[...], sc_ref[0], sh_ref[0])
        xn_ref[...] = xn.astype(xn_ref.dtype)

    o_ref[0] = jnp.dot(xn_ref[...], w_ref[...], preferred_element_type=F32,
                       precision=precision).astype(o_ref.dtype)


def _norm_proj(x, nw, sc, sh, w, out_dtype, tm, tn, name, precision=None):
    b, s, d = x.shape
    n = w.shape[1]
    tm = min(tm, s)
    tn = min(tn, n)
    return pl.pallas_call(
        functools.partial(_norm_proj_kernel, precision=precision),
        grid=(b, s // tm, n // tn),
        in_specs=[pl.BlockSpec((1, tm, d), lambda bi, m, j: (bi, m, 0)),
                  pl.BlockSpec((1, d), lambda bi, m, j: (0, 0)),
                  pl.BlockSpec((1, 1, d), lambda bi, m, j: (bi, 0, 0)),
                  pl.BlockSpec((1, 1, d), lambda bi, m, j: (bi, 0, 0)),
                  pl.BlockSpec((d, tn), lambda bi, m, j: (0, j))],
        out_specs=pl.BlockSpec((1, tm, tn), lambda bi, m, j: (bi, m, j)),
        out_shape=jax.ShapeDtypeStruct((b, s, n), out_dtype),
        scratch_shapes=[pltpu.VMEM((tm, d), w.dtype)],
        compiler_params=_cparams(("parallel", "parallel", "arbitrary")),
        name=name,
    )(x, nw, sc, sh, w)


def _norm_proj_k_dt_kernel(x_ref, nw_ref, sc_ref, sh_ref, wk_ref, wdt_ref, k_ref, dt_ref):
    xn = _modulated_norm(x_ref[0], nw_ref[...], sc_ref[0], sh_ref[0])
    k_ref[0] = jnp.dot(xn.astype(wk_ref.dtype), wk_ref[...],
                       preferred_element_type=F32).astype(k_ref.dtype)
    dt_ref[0] = jnp.dot(xn, wdt_ref[...], preferred_element_type=F32, precision=HIGHEST)


def _norm_proj_k_dt(x, nw, sc, sh, wk, wdt, tm):
    b, s, d = x.shape
    tm = min(tm, s)
    nk, ndt = wk.shape[1], wdt.shape[1]
    tok = lambda bi, m: (bi, m, 0)
    vec = lambda bi, m: (bi, 0, 0)
    const2 = lambda bi, m: (0, 0)
    return pl.pallas_call(
        _norm_proj_k_dt_kernel,
        grid=(b, s // tm),
        in_specs=[pl.BlockSpec((1, tm, d), tok),
                  pl.BlockSpec((1, d), const2),
                  pl.BlockSpec((1, 1, d), vec),
                  pl.BlockSpec((1, 1, d), vec),
                  pl.BlockSpec((d, nk), const2),
                  pl.BlockSpec((d, ndt), const2)],
        out_specs=[pl.BlockSpec((1, tm, nk), tok), pl.BlockSpec((1, tm, ndt), tok)],
        out_shape=[jax.ShapeDtypeStruct((b, s, nk), wk.dtype),
                   jax.ShapeDtypeStruct((b, s, ndt), F32)],
        compiler_params=_cparams(("parallel", "parallel")),
        name="proj_k_dt",
    )(x, nw, sc, sh, wk, wdt)


def _norm_proj_t_kernel(x_ref, nw_ref, sc_ref, sh_ref, wt_ref, rs_ref, o_ref, xn_ref):
    @pl.when(pl.program_id(2) == 0)
    def _():
        xn = _modulated_norm(x_ref[0], nw_ref[...], sc_ref[0], sh_ref[0])
        xn_ref[...] = xn.astype(xn_ref.dtype)

    o_ref[0] = (_dot_nt(wt_ref[...], xn_ref[...]) * rs_ref[...]).astype(o_ref.dtype)


def _norm_proj_t(x, nw, sc, sh, wt, row_scale, out_dtype, tm, tn, name):
    b, s, d = x.shape
    n = wt.shape[0]
    tm = min(tm, s)
    tn = min(tn, n)
    return pl.pallas_call(
        _norm_proj_t_kernel,
        grid=(b, s // tm, n // tn),
        in_specs=[pl.BlockSpec((1, tm, d), lambda bi, m, j: (bi, m, 0)),
                  pl.BlockSpec((1, d), lambda bi, m, j: (0, 0)),
                  pl.BlockSpec((1, 1, d), lambda bi, m, j: (bi, 0, 0)),
                  pl.BlockSpec((1, 1, d), lambda bi, m, j: (bi, 0, 0)),
                  pl.BlockSpec((tn, d), lambda bi, m, j: (j, 0)),
                  pl.BlockSpec((tn, 1), lambda bi, m, j: (j, 0))],
        out_specs=pl.BlockSpec((1, tn, tm), lambda bi, m, j: (bi, j, m)),
        out_shape=jax.ShapeDtypeStruct((b, n, s), out_dtype),
        scratch_shapes=[pltpu.VMEM((tm, d), wt.dtype)],
        compiler_params=_cparams(("parallel", "parallel", "arbitrary")),
        name=name,
    )(x, nw, sc, sh, wt, row_scale)


CONV_HALO = SUBLANES
CONV_COLS = 512
GROUP_COLS = D_INNER // SSD_GROUPS
HEADS_PER_GROUP = SSD_HEADS // SSD_GROUPS
PF_COLS = CONV_CH + D_INNER + 2 * D_MODEL
PF_Z_BLOCK = CONV_CH // D_INNER
PF_GS_BLOCK = (CONV_CH + D_INNER) // D_MODEL
PF_GA_BLOCK = PF_GS_BLOCK + 1


def _softplus(x):
    return jnp.maximum(x, 0.0) + jnp.log(1.0 + jnp.exp(-jnp.abs(x)))


def _ssd_kernel(xbc_ref, z_ref, dt_ref, cw_ref, cb_ref, dtb_ref, alog_ref, dsk_ref, nw_ref,
                o_ref, state_ref, win_ref, act_ref):
    L = CHUNK

    @pl.when(pl.program_id(1) == 0)
    def _():
        state_ref[...] = jnp.zeros_like(state_ref)
        win_ref[0:CONV_HALO, :] = jnp.zeros((CONV_HALO, CONV_CH), F32)

    win_ref[CONV_HALO:CONV_HALO + L, :] = xbc_ref[0]
    base = CONV_HALO - (CONV_K - 1)
    for j in range(CONV_CH // CONV_COLS):
        cs = slice(j * CONV_COLS, (j + 1) * CONV_COLS)
        acc = cb_ref[:, cs] + cw_ref[0:1, cs] * win_ref[base:base + L, cs]
        for k in range(1, CONV_K):
            acc = acc + cw_ref[k:k + 1, cs] * win_ref[base + k:base + k + L, cs]
        act_ref[:, cs] = _silu(acc)
    win_ref[0:CONV_HALO, :] = win_ref[L:L + CONV_HALO, :]

    dtv = _softplus(dt_ref[0] + dtb_ref[...])
    a = -jnp.exp(alog_ref[...])
    row = lax.broadcasted_iota(I32, (L, L), 0)
    col = lax.broadcasted_iota(I32, (L, L), 1)
    tril = row >= col
    a_cum = jnp.dot(tril.astype(F32), dtv * a, preferred_element_type=F32,
                    precision=HIGHEST)
    a_cum_t = a_cum.T
    dt_t = dtv.T
    a_last = a_cum[L - 1:L, :]
    to_end = jnp.exp(a_last - a_cum) * dtv
    ea = jnp.exp(a_cum)
    cdec = jnp.exp(a_last)

    head_of_col = lax.broadcasted_iota(I32, (L, GROUP_COLS), 1) // SSD_HEAD_DIM
    neg_inf = jnp.float32(-jnp.inf)

    for g in range(SSD_GROUPS):
        xs = slice(g * GROUP_COLS, (g + 1) * GROUP_COLS)
        bs = slice(D_INNER + g * SSD_STATE, D_INNER + (g + 1) * SSD_STATE)
        cs = slice(D_INNER + (SSD_GROUPS + g) * SSD_STATE,
                   D_INNER + (SSD_GROUPS + g + 1) * SSD_STATE)
        xg = act_ref[:, xs]
        bg = act_ref[:, bs]
        xb = xg.astype(BF16)
        bb = bg.astype(BF16)
        cb16 = act_ref[:, cs].astype(BF16)
        cbm = _dot_nt(cb16, bb)
        prev_t = state_ref[:, xs]
        y_off = jnp.dot(cb16, prev_t.astype(BF16), preferred_element_type=F32)

        y = jnp.zeros((L, GROUP_COLS), F32)
        ea_g = jnp.zeros((L, GROUP_COLS), F32)
        te_g = jnp.zeros((L, GROUP_COLS), F32)
        cd_g = jnp.zeros((1, GROUP_COLS), F32)
        for r in range(HEADS_PER_GROUP):
            h = g * HEADS_PER_GROUP + r
            seg = a_cum[:, h:h + 1] - a_cum_t[h:h + 1, :]
            decay = jnp.exp(jnp.where(tril, seg, neg_inf))
            wts = (cbm * decay * dt_t[h:h + 1, :]).astype(BF16)
            yd = jnp.dot(wts, xb, preferred_element_type=F32)
            sel = head_of_col == r
            y = jnp.where(sel, yd, y)
            ea_g = jnp.where(sel, ea[:, h:h + 1], ea_g)
            te_g = jnp.where(sel, to_end[:, h:h + 1], te_g)
            cd_g = jnp.where(sel[0:1, :], cdec[:, h:h + 1], cd_g)
        y = y + y_off * ea_g + dsk_ref[:, xs] * xg
        xs_scaled = (xg * te_g).astype(BF16)
        st = jnp.dot(bg.T.astype(BF16), xs_scaled, preferred_element_type=F32)
        state_ref[:, xs] = prev_t * cd_g + st

        yz = y * _silu(z_ref[0, :, xs])
        ms = jnp.mean(yz * yz, axis=-1, keepdims=True)
        o_ref[0, :, xs] = (yz * lax.rsqrt(ms + EPS) * nw_ref[:, xs]).astype(o_ref.dtype)


def _ssd(pf, dtraw, cw, cb, dtb, alog, dsk, nw):
    b, s, _ = pf.shape
    nc = s // CHUNK
    const2 = lambda bi, c: (0, 0)
    return pl.pallas_call(
        _ssd_kernel,
        grid=(b, nc),
        in_specs=[pl.BlockSpec((1, CHUNK, CONV_CH), lambda bi, c: (bi, c, 0)),
                  pl.BlockSpec((1, CHUNK, D_INNER), lambda bi, c: (bi, c, PF_Z_BLOCK)),
                  pl.BlockSpec((1, CHUNK, LANES), lambda bi, c: (bi, c, 0)),
                  pl.BlockSpec((CONV_K, CONV_CH), const2),
                  pl.BlockSpec((1, CONV_CH), const2),
                  pl.BlockSpec((1, LANES), const2),
                  pl.BlockSpec((1, LANES), const2),
                  pl.BlockSpec((1, D_INNER), const2),
                  pl.BlockSpec((1, D_INNER), const2)],
        out_specs=pl.BlockSpec((1, CHUNK, D_INNER), lambda bi, c: (bi, c, 0)),
        out_shape=jax.ShapeDtypeStruct((b, s, D_INNER), BF16),
        scratch_shapes=[pltpu.VMEM((SSD_STATE, D_INNER), F32),
                        pltpu.VMEM((CONV_HALO + CHUNK, CONV_CH), F32),
                        pltpu.VMEM((CHUNK, CONV_CH), F32)],
        compiler_params=_cparams(("parallel", "arbitrary")),
        name="ssd",
    )(pf, pf, dtraw, cw, cb, dtb, alog, dsk, nw)


ATT_TILE = 512
ATT_HEADS_PER_STEP = 2
LOG2E = math.log2(math.e)
BF16_ROWS = 2 * SUBLANES
ACC_ROWS = ATTN_V_DIM + BF16_ROWS


def _t5_bias_lut(rb_row):
    n = lax.broadcasted_iota(I32, (LANES, LANES), 1)
    bkt = lax.broadcasted_iota(I32, (LANES, LANES), 0)
    max_exact = NUM_BUCKETS // 2
    nf = jnp.maximum(n, 1).astype(F32)
    large = max_exact + (jnp.log(nf / max_exact) / math.log(MAX_DISTANCE / max_exact)
                         * (NUM_BUCKETS - max_exact)).astype(I32)
    large = jnp.minimum(large, NUM_BUCKETS - 1)
    bucket = jnp.where(n < max_exact, n, large)
    onehot = (bucket == bkt).astype(F32)
    lut = jnp.dot(jnp.broadcast_to(rb_row, (SUBLANES, LANES)), onehot,
                  preferred_element_type=F32, precision=HIGHEST)
    return lut[0:1, :]


def _attn_kernel(pmin_ref, pmax_ref, qt_ref, k_ref, vt_ref, pos_ref, rb_ref,
                 lamv_ref, nw_ref, o_ref, qm_ref, m_ref, acc_ref, sa_ref, sb_ref, *, tile, nh):
    T = tile
    HW = ATTN_V_DIM
    bi = pl.program_id(0)
    qi = pl.program_id(2)
    neg_inf = jnp.float32(-jnp.inf)
    nck = T // LANES

    feat = lax.broadcasted_iota(I32, (HW, T), 0)
    for hh in range(nh):
        qt = qt_ref[0, hh * HW:(hh + 1) * HW, :]
        zero = jnp.zeros_like(qt)
        qm_ref[hh, 0] = jnp.where(feat < ATTN_HEAD_DIM, qt, zero)
        qm_ref[hh, 1] = jnp.where(feat >= ATTN_HEAD_DIM, qt, zero)
    m_ref[...] = jnp.full_like(m_ref, neg_inf)
    acc_ref[...] = jnp.zeros_like(acc_ref)
    ones_rows = (lax.broadcasted_iota(I32, (BF16_ROWS, T), 0) == 0).astype(BF16)

    luts = [_t5_bias_lut(rb_ref[hh]) * LOG2E for hh in range(nh)]
    c_far = [lut[:, LANES - 1:LANES] for lut in luts]
    lut_b = [jnp.broadcast_to(lut, (LANES, LANES)) for lut in luts]
    qoff = pl.multiple_of(qi * T, T)
    pos_q = pos_ref[0, :, pl.ds(qoff, T)]

    def scores(ki, dst_ref):
        koff = pl.multiple_of(ki * T, T)
        for hh in range(nh):
            k = k_ref[0, pl.ds(koff, T), hh * HW:(hh + 1) * HW]
            for i in range(2):
                dst_ref[hh, i] = jnp.dot(k, qm_ref[hh, i], preferred_element_type=F32)

    def consume(ki, src_ref, near, diag):
        koff = pl.multiple_of(ki * T, T)
        if near:
            dists = []
            for kc in range(nck):
                pk_row = pos_ref[0, :, pl.ds(koff + kc * LANES, LANES)]
                pk = jnp.broadcast_to(pk_row, (LANES, LANES)).T
                dists.append([jnp.clip(pos_q[:, qc * LANES:(qc + 1) * LANES] - pk, 0, LANES - 1)
                              for qc in range(nck)])
            if diag:
                kr = lax.broadcasted_iota(I32, (T, T), 0)
                qc_ = lax.broadcasted_iota(I32, (T, T), 1)
                causal = kr <= qc_
        for hh in range(nh):
            vt = jnp.concatenate([vt_ref[0, hh * HW:(hh + 1) * HW, pl.ds(koff, T)], ones_rows],
                                 axis=0)
            if near:
                bias = jnp.concatenate(
                    [jnp.concatenate([jnp.take_along_axis(lut_b[hh], d, axis=1) for d in row],
                                     axis=1) for row in dists], axis=0)
                if diag:
                    bias = jnp.where(causal, bias, neg_inf)
            for i in range(2):
                s = src_ref[hh, i]
                if near:
                    s = s + bias
                    m_tile = jnp.max(s, axis=0, keepdims=True)
                else:
                    m_tile = jnp.max(s, axis=0, keepdims=True) + c_far[hh]
                m_prev = m_ref[hh, i]
                m_new = jnp.maximum(m_prev, m_tile)
                shift = m_new if near else m_new - c_far[hh]
                p = jnp.exp2(s - shift)
                alpha = jnp.exp2(m_prev - m_new)
                acc_ref[hh, i] = alpha * acc_ref[hh, i] + jnp.dot(
                    vt, p.astype(vt.dtype), preferred_element_type=F32)
                m_ref[hh, i] = m_new

    def step(t, src_ref, dst_ref):
        near = pmin_ref[bi, qi] - pmax_ref[bi, t] < N_FAR

        @pl.when(near)
        def _():
            scores(t + 1, dst_ref)
            consume(t, src_ref, True, False)

        @pl.when(jnp.logical_not(near))
        def _():
            scores(t + 1, dst_ref)
            consume(t, src_ref, False, False)

    def body(jj, carry):
        step(2 * jj, sa_ref, sb_ref)

        @pl.when(2 * jj + 1 < qi)
        def _():
            step(2 * jj + 1, sb_ref, sa_ref)

        return carry

    scores(0, sa_ref)
    lax.fori_loop(0, (qi + 1) // 2, body, 0)

    @pl.when(qi % 2 == 0)
    def _():
        consume(qi, sa_ref, True, True)

    @pl.when(qi % 2 == 1)
    def _():
        consume(qi, sb_ref, True, True)

    lv = lamv_ref[...]
    lam = (jnp.exp(jnp.sum(lv[0:1, :] * lv[1:2, :], axis=-1, keepdims=True))
           - jnp.exp(jnp.sum(lv[2:3, :] * lv[3:4, :], axis=-1, keepdims=True))
           + LAMBDA_INIT)
    nv = ATTN_V_DIM
    for hh in range(nh):
        att = (acc_ref[hh, 0, 0:nv, :] / acc_ref[hh, 0, nv:nv + 1, :]
               - lam * (acc_ref[hh, 1, 0:nv, :] / acc_ref[hh, 1, nv:nv + 1, :]))
        ms = jnp.mean(att * att, axis=0, keepdims=True)
        y = att * lax.rsqrt(ms + EPS) * nw_ref[...] * (1.0 - LAMBDA_INIT)
        o_ref[0, :, hh * HW:(hh + 1) * HW] = y.T.astype(o_ref.dtype)


def _attn(qvt, kk, positions, rel_bias, lamv, nw_col):
    b, s, _ = kk.shape
    T = min(ATT_TILE, s)
    nq = s // T
    nh = ATT_HEADS_PER_STEP
    H = ATTN_HEADS // nh
    W = nh * ATTN_V_DIM
    pos_blk = positions.reshape(b, nq, T)
    pmin = jnp.min(pos_blk, axis=-1)
    pmax = jnp.max(pos_blk, axis=-1)
    posr = positions.reshape(b, 1, s)
    rb = jnp.zeros((ATTN_HEADS, 1, LANES), F32).at[:, 0, :NUM_BUCKETS].set(rel_bias.T)
    grid_spec = pltpu.PrefetchScalarGridSpec(
        num_scalar_prefetch=2,
        grid=(b, H, nq),
        in_specs=[pl.BlockSpec((1, W, T), lambda bi, h, qi, *_: (bi, h, qi)),
                  pl.BlockSpec((1, s, W), lambda bi, h, qi, *_: (bi, 0, h)),
                  pl.BlockSpec((1, W, s), lambda bi, h, qi, *_: (bi, H + h, 0)),
                  pl.BlockSpec((1, 1, s), lambda bi, h, qi, *_: (bi, 0, 0)),
                  pl.BlockSpec((nh, 1, LANES), lambda bi, h, qi, *_: (h, 0, 0)),
                  pl.BlockSpec((SUBLANES, LANES), lambda bi, h, qi, *_: (0, 0)),
                  pl.BlockSpec((LANES, 1), lambda bi, h, qi, *_: (0, 0))],
        out_specs=pl.BlockSpec((1, T, W), lambda bi, h, qi, *_: (bi, qi, h)),
        scratch_shapes=[pltpu.VMEM((nh, 2, LANES, T), BF16),
                        pltpu.VMEM((nh, 2, 1, T), F32),
                        pltpu.VMEM((nh, 2, ACC_ROWS, T), F32),
                        pltpu.VMEM((nh, 2, T, T), F32),
                        pltpu.VMEM((nh, 2, T, T), F32)])
    return pl.pallas_call(
        functools.partial(_attn_kernel, tile=T, nh=nh),
        grid_spec=grid_spec,
        out_shape=jax.ShapeDtypeStruct((b, s, ATTN_HEADS * ATTN_V_DIM), BF16),
        compiler_params=_cparams(("parallel", "parallel", "arbitrary")),
        name="attn",
    )(pmin, pmax, qvt, kk, qvt, posr, rb, lamv, nw_col)


def _merge_kernel(x_ref, ys_ref, ya_ref, gs_ref, ga_ref, g1_ref, ws_ref, wa_ref, wo_ref, o_ref):
    y_ssd = jnp.dot(ys_ref[0], ws_ref[...], preferred_element_type=F32)
    y_att = jnp.dot(ya_ref[0], wa_ref[...], preferred_element_type=F32)
    mix_in = _sigmoid(gs_ref[0]) * y_ssd + _sigmoid(ga_ref[0]) * y_att
    mix = jnp.dot(mix_in.astype(BF16), wo_ref[...], preferred_element_type=F32)
    o_ref[0] = x_ref[0] + g1_ref[0] * mix


def _merge(x, ys, ya, pf, g1, ws, wa, wo, tm=512):
    b, s, d = x.shape
    tm = min(tm, s)
    const2 = lambda bi, m: (0, 0)
    return pl.pallas_call(
        _merge_kernel,
        grid=(b, s // tm),
        in_specs=[pl.BlockSpec((1, tm, d), lambda bi, m: (bi, m, 0)),
                  pl.BlockSpec((1, tm, D_INNER), lambda bi, m: (bi, m, 0)),
                  pl.BlockSpec((1, tm, d), lambda bi, m: (bi, m, 0)),
                  pl.BlockSpec((1, tm, d), lambda bi, m: (bi, m, PF_GS_BLOCK)),
                  pl.BlockSpec((1, tm, d), lambda bi, m: (bi, m, PF_GA_BLOCK)),
                  pl.BlockSpec((1, 1, d), lambda bi, m: (bi, 0, 0)),
                  pl.BlockSpec((D_INNER, d), const2),
                  pl.BlockSpec((d, d), const2),
                  pl.BlockSpec((d, d), const2)],
        out_specs=pl.BlockSpec((1, tm, d), lambda bi, m: (bi, m, 0)),
        out_shape=jax.ShapeDtypeStruct((b, s, d), F32),
        compiler_params=_cparams(("parallel", "parallel")),
        name="merge",
    )(x, ys, ya, pf, pf, g1, ws, wa, wo)


TOPK_TOKENS = 256
N_INST = 2 * PEER_HEADS
PAIR_CANDS = [(a, c) for a in range(PEER_TOPK) for c in range(PEER_TOPK)
              if (a + 1) * (c + 1) <= PEER_TOPK]


def _topk_kernel(q_ref, kp_ref, e1_ref, e2_ref, g_ref, s_ref, v_ref, i_ref, cs_ref, ce_ref,
                 os_ref, oe_ref):
    tn = q_ref.shape[0]
    neg_inf = jnp.float32(-jnp.inf)
    s_ref[...] = _dot_nt(kp_ref[...], q_ref[...].astype(BF16))

    def level1(it, prev):
        best = jnp.full((N_INST, tn), neg_inf, F32)
        bidx = jnp.zeros((N_INST, tn), I32)
        for key in range(N_KEYS):
            rows = slice(key * N_INST, (key + 1) * N_INST)
            sk = jnp.where(prev == key, neg_inf, s_ref[rows, :])
            s_ref[rows, :] = sk
            upd = sk > best
            best = jnp.where(upd, sk, best)
            bidx = jnp.where(upd, key, bidx)
        v_ref[it] = best
        i_ref[it] = bidx
        return bidx

    lax.fori_loop(0, PEER_TOPK, level1, jnp.full((N_INST, tn), -1, I32))

    H = PEER_HEADS
    for ci, (a, c) in enumerate(PAIR_CANDS):
        cs_ref[ci] = v_ref[a, 0:H, :] + v_ref[c, H:2 * H, :]
        ce_ref[ci] = i_ref[a, 0:H, :] * N_KEYS + i_ref[c, H:2 * H, :]

    def level2(it, prev):
        best = jnp.full((H, tn), neg_inf, F32)
        bci = jnp.zeros((H, tn), I32)
        bex = jnp.zeros((H, tn), I32)
        for ci in range(len(PAIR_CANDS)):
            sk = jnp.where(prev == ci, neg_inf, cs_ref[ci])
            cs_ref[ci] = sk
            upd = sk > best
            best = jnp.where(upd, sk, best)
            bci = jnp.where(upd, ci, bci)
            bex = jnp.where(upd, ce_ref[ci], bex)
        os_ref[it] = best
        oe_ref[it] = bex
        return bci

    lax.fori_loop(0, PEER_TOPK, level2, jnp.full((H, tn), -1, I32))

    sc = os_ref[...]
    ex = jnp.exp(sc - sc[0:1])
    gate = ex / jnp.sum(ex, axis=0, keepdims=True)
    ex_i = oe_ref[...]
    rows = PEER_TOPK * H
    g_ref[...] = gate.reshape(rows, tn).T
    key_bits = N_KEYS.bit_length() - 1
    e1 = lax.shift_right_logical(ex_i, key_bits).astype(F32).reshape(rows, tn)
    e2 = lax.bitwise_and(ex_i, N_KEYS - 1).astype(F32).reshape(rows, tn)
    e1_ref[...] = e1.T.astype(I32)
    e2_ref[...] = e2.T.astype(I32)


def _peer_topk(q, kperm):
    t, d = q.shape
    tn = min(TOPK_TOKENS, t)
    lanes = PEER_TOPK * PEER_HEADS
    ncand = len(PAIR_CANDS)
    out = jax.ShapeDtypeStruct((t, lanes), I32)
    return pl.pallas_call(
        _topk_kernel,
        grid=(t // tn,),
        in_specs=[pl.BlockSpec((tn, d), lambda i: (i, 0)),
                  pl.BlockSpec(kperm.shape, lambda i: (0, 0))],
        out_specs=[pl.BlockSpec((tn, lanes), lambda i: (i, 0))] * 3,
        out_shape=[out, out, jax.ShapeDtypeStruct((t, lanes), F32)],
        scratch_shapes=[pltpu.VMEM((N_KEYS * N_INST, tn), F32),
                        pltpu.VMEM((PEER_TOPK, N_INST, tn), F32),
                        pltpu.VMEM((PEER_TOPK, N_INST, tn), I32),
                        pltpu.VMEM((ncand, PEER_HEADS, tn), F32),
                        pltpu.VMEM((ncand, PEER_HEADS, tn), I32),
                        pltpu.VMEM((PEER_TOPK, PEER_HEADS, tn), F32),
                        pltpu.VMEM((PEER_TOPK, PEER_HEADS, tn), I32)],
        compiler_params=_cparams(("parallel",)),
        name="peer_topk",
    )(q, kperm)


MIX_TOKENS = 512
MIX_EXPERTS = 2048
MIX_VMEM_LIMIT = 56 * 1024 * 1024
MIX_SUB = 2 * N_KEYS
G_HALF = N_KEYS // 2
G_PITCH = G_HALF + SUBLANES
G_BUILD_UNROLL = 32


def _gelu_exact(x):
    return 0.5 * x * (1.0 + lax.erf(x * (1.0 / math.sqrt(2.0))))


def _mix_kernel(h_ref, nw_ref, sc_ref, sh_ref, g2_ref, fw_ref, e1_ref, e2_ref, gt_ref,
                ut_ref, v_ref, o_ref, xb_ref, g_ref, acc_ref):
    tb = h_ref.shape[1]
    c = pl.program_id(2)
    hi_mask = jnp.uint32(0xFFFF0000)

    @pl.when(c == 0)
    def _():
        xn = _modulated_norm(h_ref[0], nw_ref[...], sc_ref[0], sh_ref[0])
        xb_ref[...] = xn.astype(BF16)
        acc_ref[...] = jnp.zeros_like(acc_ref)
        sub = lax.broadcasted_iota(I32, (N_KEYS, LANES), 0)
        row_key = jnp.where(sub < G_HALF, 2 * sub, 2 * (sub - G_HALF) + 1).astype(F32).astype(BF16)
        sub_b = sub.astype(F32).astype(BF16)
        zero_b = jnp.zeros((N_KEYS, LANES), BF16)
        one_b = jnp.ones((N_KEYS, LANES), BF16)

        def row_b(ref, t):
            return jnp.broadcast_to(ref[pl.ds(t, 1), :].astype(F32), (N_KEYS, LANES)).astype(BF16)

        def build(t, carry):
            a = jnp.where(row_key == row_b(e1_ref, t), row_b(gt_ref, t), zero_b)
            bm = jnp.where(sub_b == row_b(e2_ref, t), one_b, zero_b)
            gm = _dot_nt(a, bm)
            top = lax.bitcast_convert_type(gm[0:G_HALF].astype(BF16).astype(F32), U32)
            bot = lax.bitcast_convert_type(gm[G_HALF:N_KEYS].astype(BF16).astype(F32), U32)
            g_ref[pl.ds(pl.multiple_of(t * G_PITCH, SUBLANES), G_HALF), :] = (
                top | lax.shift_right_logical(bot, jnp.uint32(16)))
            return carry

        lax.fori_loop(0, tb, build, 0, unroll=G_BUILD_UNROLL)

    xb = xb_ref[...]
    nsub = MIX_EXPERTS // MIX_SUB

    def hidden(j):
        return jnp.dot(xb, ut_ref[:, j * MIX_SUB:(j + 1) * MIX_SUB], preferred_element_type=F32)

    hid = hidden(0)
    acc = acc_ref[...]
    for j in range(nsub):
        nxt = hidden(j + 1) if j + 1 < nsub else None
        words = g_ref[pl.ds(c * nsub + j, tb, stride=G_PITCH), :]
        gates = jnp.concatenate(
            [lax.bitcast_convert_type(words & hi_mask, F32),
             lax.bitcast_convert_type(lax.shift_left(words, jnp.uint32(16)), F32)], axis=1)
        p = (_gelu_exact(hid) * gates).astype(BF16)
        acc = acc + jnp.dot(p, v_ref[j * MIX_SUB:(j + 1) * MIX_SUB, :],
                            preferred_element_type=F32)
        hid = nxt
    acc_ref[...] = acc

    @pl.when(c == pl.num_programs(2) - 1)
    def _():
        h2 = h_ref[0] + g2_ref[0] * acc_ref[...]
        r = h2 * lax.rsqrt(jnp.mean(h2 * h2, axis=-1, keepdims=True) + EPS)
        o_ref[0] = r * fw_ref[...]


def _peer_mix(h, nw, sc, sh, g2, fw, e1, e2, gate, ut, v):
    b, s, d = h.shape
    tb = min(MIX_TOKENS, s)
    nb = s // tb
    ne = v.shape[0] // MIX_EXPERTS
    lanes = PEER_TOPK * PEER_HEADS
    tok = lambda bi, m, c: (bi * nb + m, 0)
    vec = lambda bi, m, c: (bi, 0, 0)
    return pl.pallas_call(
        _mix_kernel,
        grid=(b, nb, ne),
        in_specs=[pl.BlockSpec((1, tb, d), lambda bi, m, c: (bi, m, 0)),
                  pl.BlockSpec((1, d), lambda bi, m, c: (0, 0)),
                  pl.BlockSpec((1, 1, d), vec),
                  pl.BlockSpec((1, 1, d), vec),
                  pl.BlockSpec((1, 1, d), vec),
                  pl.BlockSpec((1, d), lambda bi, m, c: (0, 0)),
                  pl.BlockSpec((tb, lanes), tok),
                  pl.BlockSpec((tb, lanes), tok),
                  pl.BlockSpec((tb, lanes), tok),
                  pl.BlockSpec((d, MIX_EXPERTS), lambda bi, m, c: (0, c)),
                  pl.BlockSpec((MIX_EXPERTS, d), lambda bi, m, c: (c, 0))],
        out_specs=pl.BlockSpec((1, tb, d), lambda bi, m, c: (bi, m, 0)),
        out_shape=jax.ShapeDtypeStruct((b, s, d), F32),
        scratch_shapes=[pltpu.VMEM((tb, d), BF16),
                        pltpu.VMEM((tb * G_PITCH, LANES), U32),
                        pltpu.VMEM((tb, d), F32)],
        compiler_params=_cparams(("parallel", "parallel", "arbitrary"), MIX_VMEM_LIMIT),
        name="peer_mix",
    )(h, nw, sc, sh, g2, fw, e1, e2, gate, ut, v)


def _pad_lanes(vec, width=LANES):
    return jnp.zeros((1, width), F32).at[0, :vec.shape[0]].set(vec)


def kernel(x, c, positions, ada_w, ada_b, norm1_w, norm2_w, final_norm_w, w_in, conv_w, conv_b,
           dt_bias, a_log, d_skip, ssd_norm_w, w_ssd_o, lambda_q1, lambda_k1, lambda_q2,
           lambda_k2, attn_norm_w, w_attn_o, rel_bias, w_out, peer_wq, peer_key1, peer_key2,
           peer_u, peer_v):
    b, s, d = x.shape
    l = 0

    c8 = jnp.zeros((SUBLANES, d), F32).at[:b].set(c)
    ada = _ada(c8, ada_w[l], ada_b[l][None, :])[:b]
    sh1, sc1, g1, sh2, sc2, g2 = [m[:, None, :] for m in jnp.split(ada, 6, axis=-1)]

    wl = w_in[l]
    w_f = jnp.concatenate([wl[:, OFF_XBC:OFF_DT], wl[:, OFF_Z:OFF_XBC], wl[:, OFF_GS:IN_COLS]],
                          axis=1).astype(BF16)
    w_k = wl[:, OFF_K:OFF_V].astype(BF16)
    w_qv_t = jnp.concatenate([wl[:, OFF_Q:OFF_K], wl[:, OFF_V:OFF_GS]], axis=1).T.astype(BF16)
    w_dt = jnp.zeros((d, LANES), F32).at[:, :SSD_HEADS].set(wl[:, OFF_DT:OFF_Q])
    nw1 = norm1_w[l][None, :]

    pf = _norm_proj(x, nw1, sc1, sh1, w_f, F32, 1024, 1024, "proj_f32")
    kk, dtraw = _norm_proj_k_dt(x, nw1, sc1, sh1, w_k, w_dt, 512)
    nqk = ATTN_HEADS * 2 * ATTN_HEAD_DIM
    row_scale = jnp.concatenate([jnp.full((nqk, 1), ATTN_HEAD_DIM ** -0.5 * LOG2E, F32),
                                 jnp.ones((ATTN_HEADS * ATTN_V_DIM, 1), F32)])
    qvt = _norm_proj_t(x, nw1, sc1, sh1, w_qv_t, row_scale, BF16, 1024, 1024, "proj_qvt")

    dsk = jnp.repeat(d_skip[l], SSD_HEAD_DIM)[None, :]
    yn = _ssd(pf, dtraw, conv_w[l], conv_b[l][None, :], _pad_lanes(dt_bias[l]),
              _pad_lanes(a_log[l]), dsk, ssd_norm_w[l][None, :])

    lamv = jnp.zeros((SUBLANES, LANES), F32)
    for i, v_ in enumerate((lambda_q1, lambda_k1, lambda_q2, lambda_k2)):
        lamv = lamv.at[i, :ATTN_HEAD_DIM].set(v_[l])
    an = _attn(qvt, kk, positions, rel_bias, lamv, attn_norm_w[l][:, None])

    h1 = _merge(x, yn, an, pf, g1, w_ssd_o[l].astype(BF16), w_attn_o[l].astype(BF16),
                w_out[l].astype(BF16))

    nw2 = norm2_w[l][None, :]
    q = _norm_proj(h1, nw2, sc2, sh2, peer_wq[l].astype(BF16), F32, 1024, 1024, "peer_q")
    keys = jnp.stack([peer_key1[l], peer_key2[l]])
    kperm = jnp.einsum('ahkd,ab,hg->kahgbd', keys, jnp.eye(2, dtype=F32),
                       jnp.eye(PEER_HEADS, dtype=F32))
    kperm = kperm.reshape(N_KEYS * N_INST, PEER_HEADS * PEER_DK).astype(BF16)
    e1, e2, gate = _peer_topk(q.reshape(b * s, PEER_HEADS * PEER_DK), kperm)
    return _peer_mix(h1, nw2, sc2, sh2, g2, final_norm_w[None, :], e1, e2, gate,
                     peer_u[l].T.astype(BF16), peer_v[l].astype(BF16))
```
